```python
import math
import jax, jax.numpy as jnp
from jax import lax
import numpy as np

D_MODEL = 1024
BATCH = 8
SEQ = 4096
DEPTH = 1

M_HEADS = 4
M_HEAD_DIM = 128
M_WIDTH = M_HEADS * M_HEAD_DIM
CONV_WIDTH = 4
CHUNK = 64
D_HEADS = 4
D_HEAD_DIM = 64
D_V_DIM = 2 * D_HEAD_DIM
D_WIDTH = D_HEADS * D_V_DIM
ROPE_THETA = 500000.0
ROPE_DIM = D_HEAD_DIM // 4
Q_BLOCK = 128
N_GROUPS = 4
EXPERTS_PER_GROUP = 8
N_EXPERTS = N_GROUPS * EXPERTS_PER_GROUP
TOP_K = 2
D_EXPERT = D_MODEL // 4
EPS = 1e-6
IN_SIZES = (M_WIDTH, M_WIDTH, M_WIDTH, M_WIDTH, M_HEADS, M_HEADS,
            D_WIDTH, D_WIDTH, D_WIDTH, D_MODEL, D_MODEL)
IN_COLS = 4 * M_WIDTH + 2 * M_HEADS + 3 * D_WIDTH + 2 * D_MODEL

kernel_name = "hybrid_mlstm_diffattn_hmoe_block"


def rmsnorm(x, w):
    x32 = x.astype(jnp.float32)
    r = x32 * lax.rsqrt(jnp.mean(x32 * x32, axis=-1, keepdims=True) + EPS)
    return (r * w.astype(jnp.float32)).astype(x.dtype)


def to_heads(t, h):
    b, s, _ = t.shape
    return t.reshape(b, s, h, -1).transpose(0, 2, 1, 3)


def causal_conv(x, w, b):
    s = x.shape[1]
    xp = jnp.pad(x, ((0, 0), (CONV_WIDTH - 1, 0), (0, 0)))
    out = b
    for j in range(CONV_WIDTH):
        out = out + w[j] * xp[:, j:j + s]
    return out


def partial_rope(t, positions):
    inv_freq = ROPE_THETA ** (-jnp.arange(0, ROPE_DIM, 2, dtype=jnp.float32) / ROPE_DIM)
    ang = positions.astype(jnp.float32)[..., None] * inv_freq
    cos = jnp.cos(ang)[:, None, None].astype(t.dtype)
    sin = jnp.sin(ang)[:, None, None].astype(t.dtype)
    rot, rest = t[..., :ROPE_DIM], t[..., ROPE_DIM:]
    r1, r2 = jnp.split(rot, 2, axis=-1)
    rot = jnp.concatenate([r1 * cos - r2 * sin, r2 * cos + r1 * sin], axis=-1)
    return jnp.concatenate([rot, rest], axis=-1)


def mlstm_chunkwise(q, k, v, log_i, log_f):
    b_, h_, s_, d_ = q.shape
    nc = s_ // CHUNK
    chunks = lambda t: t.reshape(b_, h_, nc, CHUNK, *t.shape[3:]).transpose(2, 0, 1, 3, *range(4, t.ndim + 1))
    qc, kc, vc = chunks(q), chunks(k), chunks(v)
    ic, fc = chunks(log_i), chunks(log_f)
    causal = jnp.tril(jnp.ones((CHUNK, CHUNK), dtype=bool))

    def step(carry, inp):
        C, n, m = carry
        qt, kt, vt, it, ft = inp
        bcum = jnp.cumsum(ft, axis=-1)
        dmat = jnp.where(causal, bcum[..., :, None] - bcum[..., None, :] + it[..., None, :], -jnp.inf)
        m_inter = bcum + m[..., None]
        m_t = jnp.maximum(m_inter, jnp.max(dmat, axis=-1))
        s = jnp.einsum('bhtd,bhsd->bhts', qt, kt) * jnp.exp(dmat - m_t[..., None])
        inter = jnp.exp(m_inter - m_t)
        num = jnp.einsum('bhts,bhsd->bhtd', s, vt) + inter[..., None] * jnp.einsum('bhtk,bhkv->bhtv', qt, C)
        den = jnp.sum(s, axis=-1) + inter * jnp.einsum('bhtk,bhk->bht', qt, n)
        den = jnp.maximum(jnp.abs(den), jnp.exp(-m_t))
        h = num / den[..., None]
        b_last = bcum[..., -1]
        w = b_last[..., None] - bcum + it
        m_new = jnp.maximum(b_last + m, jnp.max(w, axis=-1))
        wexp = jnp.exp(w - m_new[..., None])
        decay = jnp.exp(b_last + m - m_new)
        C_new = decay[..., None, None] * C + jnp.einsum('bhs,bhsk,bhsv->bhkv', wexp, kt, vt)
        n_new = decay[..., None] * n + jnp.einsum('bhs,bhsk->bhk', wexp, kt)
        return (C_new, n_new, m_new), h

    init = (jnp.zeros((b_, h_, d_, d_), jnp.float32), jnp.zeros((b_, h_, d_), jnp.float32),
            jnp.zeros((b_, h_), jnp.float32))
    _, hs = lax.scan(step, init, (qc, kc, vc, ic, fc))
    return hs.transpose(1, 2, 0, 3, 4).reshape(b_, h_, s_, d_)


def diff_attention(q, k, v, lam):
    b_, h_, _, s_, d_ = q.shape
    nb = s_ // Q_BLOCK
    qb = q.reshape(b_, h_, 2, nb, Q_BLOCK, d_).transpose(3, 0, 1, 2, 4, 5)
    scale = d_ ** -0.5
    kpos = jnp.arange(s_)

    def block(args):
        qi, i = args
        sc = jnp.einsum('bhmqd,bhmkd->bhmqk', qi, k).astype(jnp.float32) * scale
        qpos = i * Q_BLOCK + jnp.arange(Q_BLOCK)
        sc = jnp.where(kpos[None, :] <= qpos[:, None], sc, -jnp.inf)
        p = jax.nn.softmax(sc, axis=-1)
        a = p[:, :, 0] - lam * p[:, :, 1]
        return jnp.einsum('bhqk,bhkd->bhqd', a.astype(v.dtype), v)

    out = lax.map(block, (qb, jnp.arange(nb)))
    return out.transpose(1, 2, 0, 3, 4).reshape(b_, h_, s_, v.shape[-1])


def hierarchical_moe(h, w_rg, b_rg, w_re, b_re, w_gate, w_up, w_down):
    b_, s_, d_ = h.shape
    hf = h.reshape(-1, d_)
    g_logits = (hf @ w_rg + b_rg).astype(jnp.float32)
    pg = jax.nn.softmax(g_logits, axis=-1)
    pg_top, g_idx = lax.top_k(pg, 1)
    e_logits = (hf @ w_re + b_re).astype(jnp.float32).reshape(-1, N_GROUPS, EXPERTS_PER_GROUP)
    e_sel = jnp.take_along_axis(e_logits, g_idx[:, :, None], axis=1)[:, 0]
    pe = jax.nn.softmax(e_sel, axis=-1)
    pe_top, j_idx = lax.top_k(pe, TOP_K)
    weights = pg_top * pe_top / jnp.sum(pe_top, axis=-1, keepdims=True)
    e_idx = g_idx * EXPERTS_PER_GROUP + j_idx
    combine = jnp.sum(jax.nn.one_hot(e_idx, N_EXPERTS, dtype=jnp.float32) * weights[..., None], axis=1)
    combine = combine.astype(h.dtype)
    y = jnp.zeros_like(hf)
    for e in range(N_EXPERTS):
        act = jax.nn.silu(hf @ w_gate[e]) * (hf @ w_up[e])
        y = y + combine[:, e:e + 1] * (act @ w_down[e])
    return y.reshape(b_, s_, d_)


def setup_inputs(seed: int = 0) -> dict:
    key = jax.random.key(seed)
    ks = iter(jax.random.split(key, 40))
    nrm = lambda shape, s: jax.random.normal(next(ks), shape, jnp.float32) * s
    L = DEPTH
    x = nrm((BATCH, SEQ, D_MODEL), 1.0)
    c = nrm((BATCH, D_MODEL), 1.0)
    positions = (jnp.arange(SEQ, dtype=jnp.int32)[None, :]
                 + jax.random.randint(next(ks), (BATCH, 1), 0, 1024, dtype=jnp.int32))
    f_bias = jnp.broadcast_to(jnp.linspace(3.0, 6.0, M_HEADS, dtype=jnp.float32), (L, M_HEADS))
    return {
        "x": x, "c": c, "positions": positions,
        "w_ada": nrm((L, D_MODEL, 6 * D_MODEL), 0.5 * D_MODEL ** -0.5),
        "b_ada": nrm((L, 6 * D_MODEL), 0.02),
        "norm1_w": 1.0 + nrm((L, D_MODEL), 0.02),
        "w_in": nrm((L, D_MODEL, IN_COLS), D_MODEL ** -0.5),
        "b_igate": nrm((L, M_HEADS), 0.1),
        "b_fgate": f_bias + nrm((L, M_HEADS), 0.1),
        "conv_w": nrm((L, CONV_WIDTH, 2 * M_WIDTH), CONV_WIDTH ** -0.5),
        "conv_b": nrm((L, 2 * M_WIDTH), 0.02),
        "mlstm_norm_w": 1.0 + nrm((L, M_HEADS, M_HEAD_DIM), 0.02),
        "q_norm_w": 1.0 + nrm((L, D_HEAD_DIM), 0.02),
        "k_norm_w": 1.0 + nrm((L, D_HEAD_DIM), 0.02),
        "lam_q1": nrm((L, D_HEAD_DIM), 0.1),
        "lam_k1": nrm((L, D_HEAD_DIM), 0.1),
        "lam_q2": nrm((L, D_HEAD_DIM), 0.1),
        "lam_k2": nrm((L, D_HEAD_DIM), 0.1),
        "subln_w": 1.0 + nrm((L, D_V_DIM), 0.02),
        "w_br_m": nrm((L, M_WIDTH, D_MODEL), M_WIDTH ** -0.5),
        "w_br_d": nrm((L, D_WIDTH, D_MODEL), D_WIDTH ** -0.5),
        "w_out": nrm((L, D_MODEL, D_MODEL), D_MODEL ** -0.5),
        "norm2_w": 1.0 + nrm((L, D_MODEL), 0.02),
        "w_rg": nrm((L, D_MODEL, N_GROUPS), D_MODEL ** -0.5),
        "b_rg": nrm((L, N_GROUPS), 0.01),
        "w_re": nrm((L, D_MODEL, N_EXPERTS), D_MODEL ** -0.5),
        "b_re": nrm((L, N_EXPERTS), 0.01),
        "w_gate": nrm((L, N_EXPERTS, D_MODEL, D_EXPERT), D_MODEL ** -0.5),
        "w_up": nrm((L, N_EXPERTS, D_MODEL, D_EXPERT), D_MODEL ** -0.5),
        "w_down": nrm((L, N_EXPERTS, D_EXPERT, D_MODEL), D_EXPERT ** -0.5),
    }


def reference(x, c, positions, w_ada, b_ada, norm1_w, w_in, b_igate, b_fgate, conv_w, conv_b,
              mlstm_norm_w, q_norm_w, k_norm_w, lam_q1, lam_k1, lam_q2, lam_k2, subln_w,
              w_br_m, w_br_d, w_out, norm2_w, w_rg, b_rg, w_re, b_re, w_gate, w_up, w_down):
    split_idx = [int(v) for v in np.cumsum(IN_SIZES)[:-1]]
    for l in range(DEPTH):
        mod = (c @ w_ada[l] + b_ada[l])[:, None, :]
        shift1, scale1, gate1, shift2, scale2, gate2 = jnp.split(mod, 6, axis=-1)

        h = rmsnorm(x, norm1_w[l]) * (1.0 + scale1) + shift1
        proj = h @ w_in[l]
        (q_m, k_m, v_m, o_m, i_pre, f_pre, q_d, k_d, v_d, g_m, g_d) = jnp.split(proj, split_idx, axis=-1)

        qk = jax.nn.silu(causal_conv(jnp.concatenate([q_m, k_m], axis=-1), conv_w[l], conv_b[l]))
        q_m, k_m = jnp.split(qk, 2, axis=-1)
        qh = to_heads(q_m, M_HEADS).astype(jnp.float32)
        kh = to_heads(k_m, M_HEADS).astype(jnp.float32) * (M_HEAD_DIM ** -0.5)
        vh = to_heads(v_m, M_HEADS).astype(jnp.float32)
        log_i = (i_pre + b_igate[l]).astype(jnp.float32).transpose(0, 2, 1)
        log_f = jax.nn.log_sigmoid((f_pre + b_fgate[l]).astype(jnp.float32)).transpose(0, 2, 1)
        hm = mlstm_chunkwise(qh, kh, vh, log_i, log_f).astype(x.dtype)
        hm = rmsnorm(hm.transpose(0, 2, 1, 3), mlstm_norm_w[l])
        hm = hm.reshape(x.shape[0], x.shape[1], M_WIDTH) * jax.nn.sigmoid(o_m)

        b_, s_ = x.shape[0], x.shape[1]
        qd = q_d.reshape(b_, s_, D_HEADS, 2, D_HEAD_DIM).transpose(0, 2, 3, 1, 4)
        kd = k_d.reshape(b_, s_, D_HEADS, 2, D_HEAD_DIM).transpose(0, 2, 3, 1, 4)
        qd = partial_rope(rmsnorm(qd, q_norm_w[l]), positions)
        kd = partial_rope(rmsnorm(kd, k_norm_w[l]), positions)
        vd = to_heads(v_d, D_HEADS)
        lam_init = 0.8 - 0.6 * math.exp(-0.3 * l)
        lam = (jnp.exp(jnp.sum(lam_q1[l] * lam_k1[l])) - jnp.exp(jnp.sum(lam_q2[l] * lam_k2[l]))
               + lam_init).astype(x.dtype)
        hd = diff_attention(qd, kd, vd, lam)
        hd = rmsnorm(hd.transpose(0, 2, 1, 3), subln_w[l]) * (1.0 - lam_init)
        hd = hd.reshape(b_, s_, D_WIDTH)

        merged = jax.nn.sigmoid(g_m) * (hm @ w_br_m[l]) + jax.nn.sigmoid(g_d) * (hd @ w_br_d[l])
        x = x + gate1 * (merged @ w_out[l])

        h2 = rmsnorm(x, norm2_w[l]) * (1.0 + scale2) + shift2
        x = x + gate2 * hierarchical_moe(h2, w_rg[l], b_rg[l], w_re[l], b_re[l],
                                         w_gate[l], w_up[l], w_down[l])
    return x
```

```python
import functools
import math

import numpy as np
import jax
import jax.numpy as jnp
from jax import lax
from jax.experimental import pallas as pl
from jax.experimental.pallas import tpu as pltpu

F32 = jnp.float32
BF16 = jnp.bfloat16

D_MODEL = 1024
M_HEADS = 4
M_HEAD_DIM = 128
M_WIDTH = M_HEADS * M_HEAD_DIM
CONV_WIDTH = 4
D_HEADS = 4
D_HEAD_DIM = 64
D_V_DIM = 2 * D_HEAD_DIM
D_WIDTH = D_HEADS * D_V_DIM
ROPE_THETA = 500000.0
ROPE_DIM = D_HEAD_DIM // 4
N_GROUPS = 4
EXPERTS_PER_GROUP = 8
N_EXPERTS = N_GROUPS * EXPERTS_PER_GROUP
D_EXPERT = D_MODEL // 4
EPS = 1e-6
LAM_INIT = 0.8 - 0.6 * math.exp(-0.3 * 0)

LANES = 128
SUBLANES = 8
MXU_DIM = 256
VMEM_LIMIT = 56 * 1024 * 1024

TM_IN = 512
TM_MERGE = 512
M_CHUNK = 256
A_BLOCK = 256
TM_MOE = 1024
NEG_INF = float("-inf")


def _dot(a, b):
    return jnp.dot(a, b, preferred_element_type=F32)


def _dot_nt(a, b):
    return lax.dot_general(a, b, (((1,), (1,)), ((), ())), preferred_element_type=F32)


def _dot_tn(a, b):
    return lax.dot_general(a, b, (((0,), (0,)), ((), ())), preferred_element_type=F32)


def _sigmoid(v):
    return 1.0 / (1.0 + jnp.exp(-v))


def _const_spec(shape):
    nd = len(shape)
    return pl.BlockSpec(shape, lambda *_: (0,) * nd)


def _ada_kernel(c_ref, w_ref, b_ref, o_ref):
    o_ref[...] = _dot(c_ref[...].astype(BF16), w_ref[...].astype(BF16)) + b_ref[...]


def _ada_call(c, w_ada, b_ada):
    bsz, d = c.shape
    cols = w_ada.shape[1]
    tn = 1024
    return pl.pallas_call(
        _ada_kernel,
        grid=(cols // tn,),
        in_specs=[pl.BlockSpec((bsz, d), lambda j: (0, 0)),
                  pl.BlockSpec((d, tn), lambda j: (0, j)),
                  pl.BlockSpec((1, tn), lambda j: (0, j))],
        out_specs=pl.BlockSpec((bsz, tn), lambda j: (0, j)),
        out_shape=jax.ShapeDtypeStruct((bsz, cols), F32),
        compiler_params=pltpu.CompilerParams(dimension_semantics=("arbitrary",)),
    )(c, w_ada, b_ada.reshape(1, cols))


def _in_kernel(tiles_per_seq,
               x_ref, pos_ref, mod_ref, n1w_ref,
               wqk_ref, wv_ref, wo_ref, wg_ref, wqd_ref, wkd_ref, wvd_ref, wgm_ref, wgd_ref,
               cw_ref, cb_ref, gb_ref, qnw_ref, knw_ref, freq_ref, bd_ref,
               qm_ref, km_ref, vm_ref, so_ref, gates_ref, qd_ref, kd_ref, vd_ref, sgm_ref, sgd_ref,
               cbuf):
    tm = x_ref.shape[0]
    i = pl.program_id(0)

    x = x_ref[...]
    ms = jnp.mean(x * x, axis=-1, keepdims=True)
    scale1 = mod_ref[0, 1:2, :]
    shift1 = mod_ref[0, 0:1, :]
    h = (x * lax.rsqrt(ms + EPS) * n1w_ref[...]) * (1.0 + scale1) + shift1
    hb = h.astype(BF16)

    y = _dot(hb, wqk_ref[...])

    @pl.when(i % tiles_per_seq == 0)
    def _():
        cbuf[0:SUBLANES, :] = jnp.zeros((SUBLANES, 2 * M_WIDTH), F32)

    cbuf[SUBLANES:SUBLANES + tm, :] = y
    acc = cb_ref[...] + cw_ref[3:4, :] * y
    for d in range(1, CONV_WIDTH):
        acc = acc + cw_ref[3 - d:4 - d, :] * cbuf[SUBLANES - d:SUBLANES - d + tm, :]
    cbuf[0:SUBLANES, :] = cbuf[tm:tm + SUBLANES, :]
    a = acc * _sigmoid(acc)
    qm_ref[...] = a[:, :M_WIDTH].astype(BF16)
    km_ref[...] = (a[:, M_WIDTH:] * (M_HEAD_DIM ** -0.5)).astype(BF16)

    vm_ref[...] = _dot(hb, wv_ref[...]).astype(BF16)
    so_ref[...] = _sigmoid(_dot(hb, wo_ref[...])).astype(BF16)

    g = _dot(hb, wg_ref[...]) + gb_ref[...]
    lane = lax.broadcasted_iota(jnp.int32, g.shape, 1)
    logsig = jnp.minimum(g, 0.0) - jnp.log(1.0 + jnp.exp(-jnp.abs(g)))
    gates_ref[...] = jnp.where(lane < M_HEADS, g, logsig)

    ang = pos_ref[...].astype(F32) * freq_ref[...]
    cos = jnp.cos(ang)
    sin = jnp.sin(ang)
    lane_in_map = lax.broadcasted_iota(jnp.int32, ang.shape, 1) % D_HEAD_DIM
    half = ROPE_DIM // 2
    s_from_lo = jnp.where(lane_in_map >= half, sin, 0.0)
    s_from_hi = jnp.where(lane_in_map < half, -sin, 0.0)
    reps = D_WIDTH // LANES
    cos_w = jnp.concatenate([cos] * reps, axis=1)
    slo_w = jnp.concatenate([s_from_lo] * reps, axis=1)
    shi_w = jnp.concatenate([s_from_hi] * reps, axis=1)

    def qk_norm_rope(w_ref, nw_ref, out_scale):
        t = _dot(hb, w_ref[...])
        sq = (t * t).astype(BF16)
        gs = [_dot(sq[:, c * MXU_DIM:(c + 1) * MXU_DIM], bd_ref[...]) for c in range(D_WIDTH // MXU_DIM)]
        msq = jnp.concatenate(gs, axis=1) * (1.0 / D_HEAD_DIM)
        tn = t * lax.rsqrt(msq + EPS) * nw_ref[...]
        rot = (tn * cos_w + pltpu.roll(tn, half, 1) * slo_w
               + pltpu.roll(tn, D_WIDTH - half, 1) * shi_w)
        return (rot * out_scale).astype(BF16)

    qd_ref[...] = qk_norm_rope(wqd_ref, qnw_ref, D_HEAD_DIM ** -0.5)
    kd_ref[...] = qk_norm_rope(wkd_ref, knw_ref, 1.0)
    vd_ref[...] = _dot(hb, wvd_ref[...]).astype(BF16)
    sgm_ref[...] = _sigmoid(_dot(hb, wgm_ref[...])).astype(BF16)
    sgd_ref[...] = _sigmoid(_dot(hb, wgd_ref[...])).astype(BF16)


def _in_call(x2, pos2, mod3, n1w, weights, smalls, seq):
    n, d = x2.shape
    tm = TM_IN
    tiles_per_seq = seq // tm
    row = lambda w: pl.BlockSpec((tm, w), lambda i: (i, 0))
    in_specs = [row(d), row(1),
                pl.BlockSpec((1, 6, d), lambda i: (i // tiles_per_seq, 0, 0)),
                _const_spec(n1w.shape)]
    in_specs += [_const_spec(w.shape) for w in weights]
    in_specs += [_const_spec(s.shape) for s in smalls]
    out_widths = [(M_WIDTH, BF16), (M_WIDTH, BF16), (M_WIDTH, BF16), (M_WIDTH, BF16), (LANES, F32),
                  (D_WIDTH, BF16), (D_WIDTH, BF16), (D_WIDTH, BF16), (d, BF16), (d, BF16)]
    return pl.pallas_call(
        functools.partial(_in_kernel, tiles_per_seq),
        grid=(n // tm,),
        in_specs=in_specs,
        out_specs=[row(w) for w, _ in out_widths],
        out_shape=[jax.ShapeDtypeStruct((n, w), dt) for w, dt in out_widths],
        scratch_shapes=[pltpu.VMEM((tm + 2 * SUBLANES, 2 * M_WIDTH), F32)],
        compiler_params=pltpu.CompilerParams(dimension_semantics=("arbitrary",),
                                             vmem_limit_bytes=VMEM_LIMIT),
    )(x2, pos2, mod3, n1w, *weights, *smalls)


def _mlstm_kernel(q_ref, k_ref, v_ref, so_ref, gates_ref, nw_ref, out_ref, cx_ref, m_ref):
    L = q_ref.shape[0]
    hd = M_HEAD_DIM

    @pl.when(pl.program_id(1) == 0)
    def _():
        cx_ref[...] = jnp.zeros(cx_ref.shape, F32)
        m_ref[...] = jnp.zeros(m_ref.shape, F32)

    g = gates_ref[...]
    r_i = lax.broadcasted_iota(jnp.int32, (L, L), 0)
    c_i = lax.broadcasted_iota(jnp.int32, (L, L), 1)
    causal = c_i <= r_i
    bc = jnp.dot(causal.astype(F32), g, preferred_element_type=F32,
                 precision=lax.Precision.HIGHEST)
    g_t = g.T
    bc_t = bc.T
    ones_col = jnp.where(lax.broadcasted_iota(jnp.int32, (L, hd), 1) == 0, 1.0, 0.0).astype(BF16)

    for h in range(M_HEADS):
        sl = slice(h * hd, (h + 1) * hd)
        i_col = g[:, h:h + 1]
        b_col = bc[:, M_HEADS + h:M_HEADS + h + 1]
        i_row = g_t[h:h + 1, :]
        b_row = bc_t[M_HEADS + h:M_HEADS + h + 1, :]
        m_prev = m_ref[h][0:1, 0:1]
        cx = cx_ref[h]

        dmat = jnp.where(causal, b_col + (i_row - b_row), NEG_INF)
        m_inter = b_col + m_prev
        m_t = jnp.maximum(m_inter, jnp.max(dmat, axis=1, keepdims=True))
        q = q_ref[:, sl]
        k = k_ref[:, sl]
        vx = jnp.concatenate([v_ref[:, sl], ones_col], axis=1)
        s = _dot_nt(q, k) * jnp.exp(dmat - m_t)
        inter = jnp.exp(m_inter - m_t)
        numx = _dot(s.astype(BF16), vx) + inter * _dot(q, cx.astype(BF16))
        den = jnp.maximum(jnp.abs(numx[:, hd:hd + 1]), jnp.exp(-m_t))
        hh = numx[:, :hd] * (1.0 / den)

        b_last = b_col[L - 1:L, :]
        w_col = b_last + (i_col - b_col)
        m_new = jnp.maximum(b_last + m_prev, jnp.max(w_col, axis=0, keepdims=True))
        wexp = jnp.exp(w_col - m_new)
        decay = jnp.exp(b_last + m_prev - m_new)
        kw = (k.astype(F32) * wexp).astype(BF16)
        cx_ref[h] = decay * cx + _dot_tn(kw, vx)
        m_ref[h] = jnp.broadcast_to(m_new, m_ref.shape[1:])

        hn = hh * lax.rsqrt(jnp.mean(hh * hh, axis=-1, keepdims=True) + EPS) * nw_ref[:, sl]
        out_ref[:, sl] = (hn * so_ref[:, sl].astype(F32)).astype(BF16)


def _mlstm_call(qm, km, vm, so, gates, nw, bsz, seq):
    L = M_CHUNK
    nc = seq // L
    blk = lambda w: pl.BlockSpec((L, w), lambda b, c: (b * nc + c, 0))
    return pl.pallas_call(
        _mlstm_kernel,
        grid=(bsz, nc),
        in_specs=[blk(M_WIDTH), blk(M_WIDTH), blk(M_WIDTH), blk(M_WIDTH), blk(LANES),
                  pl.BlockSpec((1, M_WIDTH), lambda b, c: (0, 0))],
        out_specs=blk(M_WIDTH),
        out_shape=jax.ShapeDtypeStruct((bsz * seq, M_WIDTH), BF16),
        scratch_shapes=[pltpu.VMEM((M_HEADS, M_HEAD_DIM, 2 * M_HEAD_DIM), F32),
                        pltpu.VMEM((M_HEADS, SUBLANES, LANES), F32)],
        compiler_params=pltpu.CompilerParams(dimension_semantics=("arbitrary", "arbitrary"),
                                             vmem_limit_bytes=VMEM_LIMIT),
    )(qm, km, vm, so, gates, nw)


def _attn_kernel(q_ref, k_ref, v_ref, lam_ref, sw_ref, out_ref):
    seq = q_ref.shape[0]
    t = A_BLOCK
    nq = seq // t
    lp = lam_ref[...]
    lam = (jnp.exp(jnp.sum(lp[0:1] * lp[1:2], axis=-1, keepdims=True))
           - jnp.exp(jnp.sum(lp[2:3] * lp[3:4], axis=-1, keepdims=True)) + LAM_INIT)
    lane = lax.broadcasted_iota(jnp.int32, (t, LANES), 1)
    r_i = lax.broadcasted_iota(jnp.int32, (t, t), 0)
    c_i = lax.broadcasted_iota(jnp.int32, (t, t), 1)
    diag_mask = c_i <= r_i

    def q_block(i, _):
        q = q_ref[pl.ds(pl.multiple_of(i * t, t), t), :]
        qs = (jnp.where(lane < D_HEAD_DIM, q, jnp.zeros_like(q)),
              jnp.where(lane >= D_HEAD_DIM, q, jnp.zeros_like(q)))

        def step(j, carry, masked):
            off = pl.multiple_of(j * t, t)
            k = k_ref[pl.ds(off, t), :]
            v = v_ref[pl.ds(off, t), :]
            new = []
            for mp in range(2):
                m, l, a = carry[3 * mp:3 * mp + 3]
                s = _dot_nt(qs[mp], k)
                if masked:
                    s = jnp.where(diag_mask, s, NEG_INF)
                mn = jnp.maximum(m, jnp.max(s, axis=-1, keepdims=True))
                alpha = jnp.exp(m - mn)
                p = jnp.exp(s - mn)
                l = alpha * l + jnp.sum(p, axis=-1, keepdims=True)
                a = alpha * a + _dot(p.astype(BF16), v)
                new += [mn, l, a]
            return tuple(new)

        init = (jnp.full((t, 1), NEG_INF, F32), jnp.zeros((t, 1), F32), jnp.zeros((t, D_V_DIM), F32)) * 2
        carry = lax.fori_loop(0, i, lambda j, c: step(j, c, False), init)
        m0, l0, a0, m1, l1, a1 = step(i, carry, True)
        o = a0 * (1.0 / l0) - lam * (a1 * (1.0 / l1))
        on = o * lax.rsqrt(jnp.mean(o * o, axis=-1, keepdims=True) + EPS) * sw_ref[...] * (1.0 - LAM_INIT)
        out_ref[pl.ds(pl.multiple_of(i * t, t), t), :] = on.astype(BF16)
        return 0

    lax.fori_loop(0, nq, q_block, 0)


def _attn_call(qd, kd, vd, lam_rows, subw, bsz, seq):
    blk = pl.BlockSpec((seq, LANES), lambda b, h: (b, h))
    return pl.pallas_call(
        _attn_kernel,
        grid=(bsz, D_HEADS),
        in_specs=[blk, blk, blk, _const_spec(lam_rows.shape), _const_spec(subw.shape)],
        out_specs=blk,
        out_shape=jax.ShapeDtypeStruct((bsz * seq, D_WIDTH), BF16),
        compiler_params=pltpu.CompilerParams(dimension_semantics=("arbitrary", "arbitrary"),
                                             vmem_limit_bytes=VMEM_LIMIT),
    )(qd, kd, vd, lam_rows, subw)


def _merge_kernel(x_ref, hm_ref, hd_ref, sgm_ref, sgd_ref, mod_ref, wbm_ref, wbd_ref, wout_ref, n2w_ref,
                  wr_hi_ref, wr_lo_ref, br_ref, x1_ref, h2_ref, comb_ref):
    gate1 = mod_ref[0, 2:3, :]
    shift2 = mod_ref[0, 3:4, :]
    scale2 = mod_ref[0, 4:5, :]
    merged = (sgm_ref[...].astype(F32) * _dot(hm_ref[...], wbm_ref[...])
              + sgd_ref[...].astype(F32) * _dot(hd_ref[...], wbd_ref[...]))
    x1 = x_ref[...] + gate1 * _dot(merged.astype(BF16), wout_ref[...])
    x1_ref[...] = x1
    ms = jnp.mean(x1 * x1, axis=-1, keepdims=True)
    h2 = (x1 * lax.rsqrt(ms + EPS) * n2w_ref[...]) * (1.0 + scale2) + shift2
    h2_hi = h2.astype(BF16)
    h2_ref[...] = h2_hi
    h2_lo = (h2 - h2_hi.astype(F32)).astype(BF16)
    logits = (_dot(h2_hi, wr_hi_ref[...]) + _dot(h2_lo, wr_hi_ref[...]) + _dot(h2_hi, wr_lo_ref[...])
              + br_ref[...])
    lane = lax.broadcasted_iota(jnp.int32, logits.shape, 1)
    lane_f = lane.astype(F32)
    big = float(LANES)

    def top(vals):
        mx = jnp.max(vals, axis=-1, keepdims=True)
        idx = jnp.min(jnp.where(vals == mx, lane_f, big), axis=-1, keepdims=True)
        return mx, idx

    is_group = lane < N_GROUPS
    gmax, gidx = top(jnp.where(is_group, logits, NEG_INF))
    gsum = jnp.sum(jnp.where(is_group, jnp.exp(logits - gmax), 0.0), axis=-1, keepdims=True)
    pg_top = 1.0 / gsum
    lo = N_GROUPS + EXPERTS_PER_GROUP * gidx
    in_group = (lane_f >= lo) & (lane_f < lo + EXPERTS_PER_GROUP)
    el = jnp.where(in_group, logits, NEG_INF)
    e1max, e1idx = top(el)
    e2max, e2idx = top(jnp.where(lane_f == e1idx, NEG_INF, el))
    p2 = jnp.exp(e2max - e1max)
    w1 = pg_top / (1.0 + p2)
    w2 = w1 * p2
    comb_ref[...] = jnp.where(lane_f == e1idx, w1, 0.0) + jnp.where(lane_f == e2idx, w2, 0.0)


def _merge_call(x2, hm, hd, sgm, sgd, mod3, wbm, wbd, wout, n2w, wr_hi, wr_lo, br, seq):
    n, d = x2.shape
    tm = TM_MERGE
    tiles_per_seq = seq // tm
    row = lambda w: pl.BlockSpec((tm, w), lambda i: (i, 0))
    consts = [wbm, wbd, wout, n2w, wr_hi, wr_lo, br]
    return pl.pallas_call(
        _merge_kernel,
        grid=(n // tm,),
        in_specs=[row(d), row(M_WIDTH), row(D_WIDTH), row(d), row(d),
                  pl.BlockSpec((1, 6, d), lambda i: (i // tiles_per_seq, 0, 0))]
                 + [_const_spec(a.shape) for a in consts],
        out_specs=[row(d), row(d), row(LANES)],
        out_shape=[jax.ShapeDtypeStruct((n, d), F32), jax.ShapeDtypeStruct((n, d), BF16),
                   jax.ShapeDtypeStruct((n, LANES), F32)],
        compiler_params=pltpu.CompilerParams(dimension_semantics=("arbitrary",),
                                             vmem_limit_bytes=VMEM_LIMIT),
    )(x2, hm, hd, sgm, sgd, mod3, *consts)


def _moe_kernel(h2_ref, comb_ref, x1_ref, mod_ref, wg_ref, wu_ref, wd_ref, out_ref, acc_ref):
    e = pl.program_id(1)

    @pl.when(e == 0)
    def _():
        acc_ref[...] = jnp.zeros(acc_ref.shape, F32)

    h2 = h2_ref[...]
    gte = _dot(h2, wg_ref[0].astype(BF16))
    up = _dot(h2, wu_ref[0].astype(BF16))
    act = (gte * _sigmoid(gte) * up).astype(BF16)
    lane = lax.broadcasted_iota(jnp.int32, comb_ref.shape, 1)
    ce = jnp.sum(jnp.where(lane == e + N_GROUPS, comb_ref[...], 0.0), axis=-1, keepdims=True)
    acc_ref[...] += ce * _dot(act, wd_ref[0].astype(BF16))

    @pl.when(e == N_EXPERTS - 1)
    def _():
        gate2 = mod_ref[0, 5:6, :]
        out_ref[...] = x1_ref[...] + gate2 * acc_ref[...]


def _moe_call(h2, comb, x1, mod3, w_gate, w_up, w_down, seq):
    n, d = x1.shape
    tm = TM_MOE
    tiles_per_seq = seq // tm
    row = lambda w: pl.BlockSpec((tm, w), lambda i, e: (i, 0))
    return pl.pallas_call(
        _moe_kernel,
        grid=(n // tm, N_EXPERTS),
        in_specs=[row(d), row(LANES), row(d),
                  pl.BlockSpec((1, 6, d), lambda i, e: (i // tiles_per_seq, 0, 0)),
                  pl.BlockSpec((1, d, D_EXPERT), lambda i, e: (e, 0, 0)),
                  pl.BlockSpec((1, d, D_EXPERT), lambda i, e: (e, 0, 0)),
                  pl.BlockSpec((1, D_EXPERT, d), lambda i, e: (e, 0, 0))],
        out_specs=row(d),
        out_shape=jax.ShapeDtypeStruct((n, d), F32),
        scratch_shapes=[pltpu.VMEM((tm, d), F32)],
        compiler_params=pltpu.CompilerParams(dimension_semantics=("arbitrary", "arbitrary"),
                                             vmem_limit_bytes=VMEM_LIMIT),
    )(h2, comb, x1, mod3, w_gate, w_up, w_down)


def _rope_freq_row():
    inv = ROPE_THETA ** (-np.arange(0, ROPE_DIM, 2, dtype=np.float32) / ROPE_DIM)
    row = np.zeros((LANES,), np.float32)
    half = ROPE_DIM // 2
    for base in range(0, LANES, D_HEAD_DIM):
        row[base:base + half] = inv
        row[base + half:base + ROPE_DIM] = inv
    return jnp.asarray(row.reshape(1, LANES))


def _block_diag_ones():
    idx = np.arange(MXU_DIM) // D_HEAD_DIM
    return jnp.asarray((idx[:, None] == idx[None, :]).astype(np.float32), dtype=BF16)


def kernel(x, c, positions, w_ada, b_ada, norm1_w, w_in, b_igate, b_fgate, conv_w, conv_b, mlstm_norm_w, q_norm_w, k_norm_w, lam_q1, lam_k1, lam_q2, lam_k2, subln_w, w_br_m, w_br_d, w_out, norm2_w, w_rg, b_rg, w_re, b_re, w_gate, w_up, w_down):
    bsz, seq, d = x.shape
    n = bsz * seq
    x2 = x.reshape(n, d)
    pos2 = positions.reshape(n, 1)
    l = 0

    mod3 = _ada_call(c, w_ada[l], b_ada[l]).reshape(bsz, 6, d)

    wi = w_in[l]
    o = 0
    segs = {}
    for name, width in (("qk", 2 * M_WIDTH), ("v", M_WIDTH), ("o", M_WIDTH), ("g", 2 * M_HEADS),
                        ("qd", D_WIDTH), ("kd", D_WIDTH), ("vd", D_WIDTH), ("gm", d), ("gd", d)):
        segs[name] = wi[:, o:o + width]
        o += width
    segs["g"] = jnp.pad(segs["g"], ((0, 0), (0, LANES - 2 * M_HEADS)))
    weights = [segs[k].astype(BF16) for k in ("qk", "v", "o", "g", "qd", "kd", "vd", "gm", "gd")]
    gate_bias = jnp.pad(jnp.concatenate([b_igate[l], b_fgate[l]]), (0, LANES - 2 * M_HEADS)).reshape(1, LANES)
    reps = D_WIDTH // D_HEAD_DIM
    smalls = [conv_w[l], conv_b[l].reshape(1, -1), gate_bias,
              jnp.tile(q_norm_w[l], reps).reshape(1, D_WIDTH), jnp.tile(k_norm_w[l], reps).reshape(1, D_WIDTH),
              _rope_freq_row(), _block_diag_ones()]
    qm, km, vm, so, gates, qd, kd, vd, sgm, sgd = _in_call(
        x2, pos2, mod3, norm1_w[l].reshape(1, d), weights, smalls, seq)

    hm = _mlstm_call(qm, km, vm, so, gates, mlstm_norm_w[l].reshape(1, M_WIDTH), bsz, seq)

    lam_rows = jnp.stack([lam_q1[l], lam_k1[l], lam_q2[l], lam_k2[l]])
    hd = _attn_call(qd, kd, vd, lam_rows, subln_w[l].reshape(1, D_V_DIM), bsz, seq)

    w_r = jnp.pad(jnp.concatenate([w_rg[l], w_re[l]], axis=1), ((0, 0), (0, LANES - N_GROUPS - N_EXPERTS)))
    wr_hi = w_r.astype(BF16)
    wr_lo = (w_r - wr_hi.astype(F32)).astype(BF16)
    b_r = jnp.pad(jnp.concatenate([b_rg[l], b_re[l]]), (0, LANES - N_GROUPS - N_EXPERTS)).reshape(1, LANES)
    x1, h2, comb = _merge_call(x2, hm, hd, sgm, sgd, mod3, w_br_m[l].astype(BF16), w_br_d[l].astype(BF16),
                               w_out[l].astype(BF16), norm2_w[l].reshape(1, d), wr_hi, wr_lo, b_r, seq)

    out = _moe_call(h2, comb, x1, mod3, w_gate[l], w_up[l], w_down[l], seq)
    return out.reshape(bsz, seq, d)
```

```python
import functools
import math

import numpy as np
import jax
import jax.numpy as jnp
from jax import lax
from jax.experimental import pallas as pl
from jax.experimental.pallas import tpu as pltpu

F32 = jnp.float32
BF16 = jnp.bfloat16

D_MODEL = 1024
M_HEADS = 4
M_HEAD_DIM = 128
M_WIDTH = M_HEADS * M_HEAD_DIM
CONV_WIDTH = 4
D_HEADS = 4
D_HEAD_DIM = 64
D_V_DIM = 2 * D_HEAD_DIM
D_WIDTH = D_HEADS * D_V_DIM
ROPE_THETA = 500000.0
ROPE_DIM = D_HEAD_DIM // 4
N_GROUPS = 4
EXPERTS_PER_GROUP = 8
N_EXPERTS = N_GROUPS * EXPERTS_PER_GROUP
D_EXPERT = D_MODEL // 4
EPS = 1e-6
LAM_INIT = 0.8 - 0.6 * math.exp(-0.3 * 0)

LANES = 128
SUBLANES = 8
MXU_DIM = 256
VMEM_LIMIT = 56 * 1024 * 1024

TM_IN = 512
TM_MERGE = 512
M_CHUNK = 256
A_BLOCK = 512
LOG2_E = math.log2(math.e)
A_UNROLL = 4
BOUND_SLACK = 1.01
TM_MOE = 1024
NEG_INF = float("-inf")


def _dot(a, b):
    return jnp.dot(a, b, preferred_element_type=F32)


def _dot_nt(a, b):
    return lax.dot_general(a, b, (((1,), (1,)), ((), ())), preferred_element_type=F32)


def _dot_tn(a, b):
    return lax.dot_general(a, b, (((0,), (0,)), ((), ())), preferred_element_type=F32)


def _sigmoid(v):
    return 1.0 / (1.0 + jnp.exp(-v))


def _const_spec(shape):
    nd = len(shape)
    return pl.BlockSpec(shape, lambda *_: (0,) * nd)


def _ada_kernel(c_ref, w_ref, b_ref, o_ref):
    o_ref[...] = _dot(c_ref[...].astype(BF16), w_ref[...].astype(BF16)) + b_ref[...]


def _ada_call(c, w_ada, b_ada):
    bsz, d = c.shape
    cols = w_ada.shape[1]
    tn = 1024
    return pl.pallas_call(
        _ada_kernel,
        grid=(cols // tn,),
        in_specs=[pl.BlockSpec((bsz, d), lambda j: (0, 0)),
                  pl.BlockSpec((d, tn), lambda j: (0, j)),
                  pl.BlockSpec((1, tn), lambda j: (0, j))],
        out_specs=pl.BlockSpec((bsz, tn), lambda j: (0, j)),
        out_shape=jax.ShapeDtypeStruct((bsz, cols), F32),
        compiler_params=pltpu.CompilerParams(dimension_semantics=("arbitrary",)),
    )(c, w_ada, b_ada.reshape(1, cols))


def _in_kernel(tiles_per_seq,
               x_ref, pos_ref, mod_ref, n1w_ref,
               wqk_ref, wv_ref, wo_ref, wg_ref, wqd_ref, wkd_ref, wvd_ref, wgm_ref, wgd_ref,
               cw_ref, cb_ref, gb_ref, qnw_ref, knw_ref, freq_ref, bd_ref,
               qm_ref, km_ref, vm_ref, so_ref, gates_ref, qd_ref, kd_ref, vd_ref, sgm_ref, sgd_ref,
               cbuf):
    tm = x_ref.shape[0]
    i = pl.program_id(0)

    x = x_ref[...]
    ms = jnp.mean(x * x, axis=-1, keepdims=True)
    scale1 = mod_ref[0, 1:2, :]
    shift1 = mod_ref[0, 0:1, :]
    h = (x * lax.rsqrt(ms + EPS) * n1w_ref[...]) * (1.0 + scale1) + shift1
    hb = h.astype(BF16)

    y = _dot(hb, wqk_ref[...])

    @pl.when(i % tiles_per_seq == 0)
    def _():
        cbuf[0:SUBLANES, :] = jnp.zeros((SUBLANES, 2 * M_WIDTH), F32)

    cbuf[SUBLANES:SUBLANES + tm, :] = y
    acc = cb_ref[...] + cw_ref[3:4, :] * y
    for d in range(1, CONV_WIDTH):
        acc = acc + cw_ref[3 - d:4 - d, :] * cbuf[SUBLANES - d:SUBLANES - d + tm, :]
    cbuf[0:SUBLANES, :] = cbuf[tm:tm + SUBLANES, :]
    a = acc * _sigmoid(acc)
    qm_ref[...] = a[:, :M_WIDTH].astype(BF16)
    km_ref[...] = (a[:, M_WIDTH:] * (M_HEAD_DIM ** -0.5)).astype(BF16)

    vm_ref[...] = _dot(hb, wv_ref[...]).astype(BF16)
    so_ref[...] = _sigmoid(_dot(hb, wo_ref[...])).astype(BF16)

    g = _dot(hb, wg_ref[...]) + gb_ref[...]
    lane = lax.broadcasted_iota(jnp.int32, g.shape, 1)
    logsig = jnp.minimum(g, 0.0) - jnp.log(1.0 + jnp.exp(-jnp.abs(g)))
    gates_ref[...] = jnp.where(lane < M_HEADS, g, logsig)

    ang = pos_ref[...].astype(F32) * freq_ref[...]
    cos = jnp.cos(ang)
    sin = jnp.sin(ang)
    lane_in_map = lax.broadcasted_iota(jnp.int32, ang.shape, 1) % D_HEAD_DIM
    half = ROPE_DIM // 2
    s_from_lo = jnp.where(lane_in_map >= half, sin, 0.0)
    s_from_hi = jnp.where(lane_in_map < half, -sin, 0.0)
    reps = D_WIDTH // LANES
    cos_w = jnp.concatenate([cos] * reps, axis=1)
    slo_w = jnp.concatenate([s_from_lo] * reps, axis=1)
    shi_w = jnp.concatenate([s_from_hi] * reps, axis=1)

    def qk_norm_rope(w_ref, nw_ref, out_scale):
        t = _dot(hb, w_ref[...])
        sq = (t * t).astype(BF16)
        gs = [_dot(sq[:, c * MXU_DIM:(c + 1) * MXU_DIM], bd_ref[...]) for c in range(D_WIDTH // MXU_DIM)]
        msq = jnp.concatenate(gs, axis=1) * (1.0 / D_HEAD_DIM)
        tn = t * lax.rsqrt(msq + EPS) * nw_ref[...]
        rot = (tn * cos_w + pltpu.roll(tn, half, 1) * slo_w
               + pltpu.roll(tn, D_WIDTH - half, 1) * shi_w)
        return (rot * out_scale).astype(BF16)

    qd_ref[...] = qk_norm_rope(wqd_ref, qnw_ref, D_HEAD_DIM ** -0.5 * LOG2_E)
    kd_ref[...] = qk_norm_rope(wkd_ref, knw_ref, 1.0)
    vd_ref[...] = _dot(hb, wvd_ref[...]).astype(BF16)
    sgm_ref[...] = _sigmoid(_dot(hb, wgm_ref[...])).astype(BF16)
    sgd_ref[...] = _sigmoid(_dot(hb, wgd_ref[...])).astype(BF16)


def _in_call(x2, pos2, mod3, n1w, weights, smalls, seq):
    n, d = x2.shape
    tm = TM_IN
    tiles_per_seq = seq // tm
    row = lambda w: pl.BlockSpec((tm, w), lambda i: (i, 0))
    in_specs = [row(d), row(1),
                pl.BlockSpec((1, 6, d), lambda i: (i // tiles_per_seq, 0, 0)),
                _const_spec(n1w.shape)]
    in_specs += [_const_spec(w.shape) for w in weights]
    in_specs += [_const_spec(s.shape) for s in smalls]
    out_widths = [(M_WIDTH, BF16), (M_WIDTH, BF16), (M_WIDTH, BF16), (M_WIDTH, BF16), (LANES, F32),
                  (D_WIDTH, BF16), (D_WIDTH, BF16), (D_WIDTH, BF16), (d, BF16), (d, BF16)]
    return pl.pallas_call(
        functools.partial(_in_kernel, tiles_per_seq),
        grid=(n // tm,),
        in_specs=in_specs,
        out_specs=[row(w) for w, _ in out_widths],
        out_shape=[jax.ShapeDtypeStruct((n, w), dt) for w, dt in out_widths],
        scratch_shapes=[pltpu.VMEM((tm + 2 * SUBLANES, 2 * M_WIDTH), F32)],
        compiler_params=pltpu.CompilerParams(dimension_semantics=("arbitrary",),
                                             vmem_limit_bytes=VMEM_LIMIT),
    )(x2, pos2, mod3, n1w, *weights, *smalls)


def _mlstm_kernel(q_ref, k_ref, v_ref, so_ref, gates_ref, nw_ref, out_ref, cx_ref, m_ref):
    L = q_ref.shape[0]
    hd = M_HEAD_DIM

    @pl.when(pl.program_id(1) == 0)
    def _():
        cx_ref[...] = jnp.zeros(cx_ref.shape, F32)
        m_ref[...] = jnp.zeros(m_ref.shape, F32)

    g = gates_ref[...]
    r_i = lax.broadcasted_iota(jnp.int32, (L, L), 0)
    c_i = lax.broadcasted_iota(jnp.int32, (L, L), 1)
    causal = c_i <= r_i
    bc = jnp.dot(causal.astype(F32), g, preferred_element_type=F32,
                 precision=lax.Precision.HIGHEST)
    g_t = g.T
    bc_t = bc.T
    ones_col = jnp.where(lax.broadcasted_iota(jnp.int32, (L, hd), 1) == 0, 1.0, 0.0).astype(BF16)

    for h in range(M_HEADS):
        sl = slice(h * hd, (h + 1) * hd)
        i_col = g[:, h:h + 1]
        b_col = bc[:, M_HEADS + h:M_HEADS + h + 1]
        i_row = g_t[h:h + 1, :]
        b_row = bc_t[M_HEADS + h:M_HEADS + h + 1, :]
        m_prev = m_ref[h][0:1, 0:1]
        cx = cx_ref[h]

        dmat = jnp.where(causal, b_col + (i_row - b_row), NEG_INF)
        m_inter = b_col + m_prev
        m_t = jnp.maximum(m_inter, jnp.max(dmat, axis=1, keepdims=True))
        q = q_ref[:, sl]
        k = k_ref[:, sl]
        vx = jnp.concatenate([v_ref[:, sl], ones_col], axis=1)
        s = _dot_nt(q, k) * jnp.exp(dmat - m_t)
        inter = jnp.exp(m_inter - m_t)
        numx = _dot(s.astype(BF16), vx) + inter * _dot(q, cx.astype(BF16))
        den = jnp.maximum(jnp.abs(numx[:, hd:hd + 1]), jnp.exp(-m_t))
        hh = numx[:, :hd] * (1.0 / den)

        b_last = b_col[L - 1:L, :]
        w_col = b_last + (i_col - b_col)
        m_new = jnp.maximum(b_last + m_prev, jnp.max(w_col, axis=0, keepdims=True))
        wexp = jnp.exp(w_col - m_new)
        decay = jnp.exp(b_last + m_prev - m_new)
        kw = (k.astype(F32) * wexp).astype(BF16)
        cx_ref[h] = decay * cx + _dot_tn(kw, vx)
        m_ref[h] = jnp.broadcast_to(m_new, m_ref.shape[1:])

        hn = hh * lax.rsqrt(jnp.mean(hh * hh, axis=-1, keepdims=True) + EPS) * nw_ref[:, sl]
        out_ref[:, sl] = (hn * so_ref[:, sl].astype(F32)).astype(BF16)


def _mlstm_call(qm, km, vm, so, gates, nw, bsz, seq):
    L = M_CHUNK
    nc = seq // L
    blk = lambda w: pl.BlockSpec((L, w), lambda b, c: (b * nc + c, 0))
    return pl.pallas_call(
        _mlstm_kernel,
        grid=(bsz, nc),
        in_specs=[blk(M_WIDTH), blk(M_WIDTH), blk(M_WIDTH), blk(M_WIDTH), blk(LANES),
                  pl.BlockSpec((1, M_WIDTH), lambda b, c: (0, 0))],
        out_specs=blk(M_WIDTH),
        out_shape=jax.ShapeDtypeStruct((bsz * seq, M_WIDTH), BF16),
        scratch_shapes=[pltpu.VMEM((M_HEADS, M_HEAD_DIM, 2 * M_HEAD_DIM), F32),
                        pltpu.VMEM((M_HEADS, SUBLANES, LANES), F32)],
        compiler_params=pltpu.CompilerParams(dimension_semantics=("arbitrary", "arbitrary"),
                                             vmem_limit_bytes=VMEM_LIMIT),
    )(qm, km, vm, so, gates, nw)


def _attn_kernel(ii_ref, jj_ref, q_ref, k_ref, v_ref, lam_ref, sw_ref, out_ref,
                 vt_ref, kp_ref, qa_ref, acc_ref, l_ref):
    seq = q_ref.shape[0]
    t = A_BLOCK
    nq = seq // t
    lp = lam_ref[...]
    lam = (jnp.exp(jnp.sum(lp[0:1] * lp[1:2], axis=-1, keepdims=True))
           - jnp.exp(jnp.sum(lp[2:3] * lp[3:4], axis=-1, keepdims=True)) + LAM_INIT)
    for jb in range(nq):
        vt_ref[jb] = v_ref[jb * t:(jb + 1) * t, :].T

    aux = (D_HEAD_DIM, 0)
    k = k_ref[...]
    ksq = jnp.square(k.astype(F32))
    lane_s = lax.broadcasted_iota(jnp.int32, (seq, LANES), 1)
    in_map_s = (lane_s < D_HEAD_DIM, lane_s >= D_HEAD_DIM)
    kmax2 = []
    for mp in range(2):
        kn2 = jnp.sum(jnp.where(in_map_s[mp], ksq, 0.0), axis=1, keepdims=True)
        kmax2.append(jnp.max(kn2, axis=0, keepdims=True))
        kp_ref[mp] = jnp.where(in_map_s[mp], k, jnp.where(lane_s == aux[mp], 1.0, 0.0).astype(BF16))

    q = q_ref[...].astype(F32)
    qsq = q * q
    for mp in range(2):
        qn2 = jnp.sum(jnp.where(in_map_s[mp], qsq, 0.0), axis=1, keepdims=True)
        bound = jnp.sqrt(qn2 * kmax2[mp]) * BOUND_SLACK
        qa_ref[mp] = jnp.where(in_map_s[mp], q, jnp.where(lane_s == aux[mp], -bound, 0.0)).astype(BF16)
    l_ref[...] = jnp.zeros(l_ref.shape, F32)
    acc_ref[...] = jnp.zeros(acc_ref.shape, F32)

    k_i = lax.broadcasted_iota(jnp.int32, (t, t), 0)
    q_i = lax.broadcasted_iota(jnp.int32, (t, t), 1)
    diag_mask = k_i <= q_i

    def step(i, j, masked):
        koff = pl.multiple_of(j * t, t)
        qoff = pl.multiple_of(i * t, t)
        vt = vt_ref[j]
        for mp in range(2):
            s = _dot_nt(kp_ref[mp, pl.ds(koff, t), :], qa_ref[mp, pl.ds(qoff, t), :])
            if masked:
                s = jnp.where(diag_mask, s, NEG_INF)
            p = jnp.exp2(s)
            l_ref[mp * nq + i] += jnp.sum(p, axis=0, keepdims=True)
            acc_ref[mp * nq + i] += _dot(vt, p.astype(BF16))

    def run(n_steps, step_at):
        trips = n_steps // A_UNROLL

        def body(u, c):
            for r in range(A_UNROLL):
                step_at(u * A_UNROLL + r)
            return c

        if trips:
            lax.fori_loop(0, trips, body, 0)
        for p in range(trips * A_UNROLL, n_steps):
            step_at(p)

    run(nq * (nq - 1) // 2, lambda p: step(ii_ref[p], jj_ref[p], False))
    run(nq, lambda p: step(p, p, True))

    def finish(i, c):
        o_t = (acc_ref[i] * (1.0 / l_ref[i]) - lam * (acc_ref[nq + i] * (1.0 / l_ref[nq + i])))
        nrm = lax.rsqrt(jnp.mean(o_t * o_t, axis=0, keepdims=True) + EPS)
        on = (o_t * nrm).T * sw_ref[...] * (1.0 - LAM_INIT)
        out_ref[pl.ds(pl.multiple_of(i * t, t), t), :] = on.astype(BF16)
        return c

    lax.fori_loop(0, nq, finish, 0)


def _attn_call(qd, kd, vd, lam_rows, subw, bsz, seq):
    t = A_BLOCK
    nq = seq // t
    pairs = [(i, j) for i in range(nq) for j in range(i)]
    ii = jnp.asarray([p[0] for p in pairs] or [0], jnp.int32)
    jj = jnp.asarray([p[1] for p in pairs] or [0], jnp.int32)
    blk = pl.BlockSpec((seq, LANES), lambda b, h, *_: (b, h))
    const = lambda shape: pl.BlockSpec(shape, lambda b, h, *_: (0,) * len(shape))
    return pl.pallas_call(
        _attn_kernel,
        grid_spec=pltpu.PrefetchScalarGridSpec(
            num_scalar_prefetch=2,
            grid=(bsz, D_HEADS),
            in_specs=[blk, blk, blk, const(lam_rows.shape), const(subw.shape)],
            out_specs=blk,
            scratch_shapes=[pltpu.VMEM((nq, D_V_DIM, t), BF16),
                            pltpu.VMEM((2, seq, LANES), BF16),
                            pltpu.VMEM((2, seq, LANES), BF16),
                            pltpu.VMEM((2 * nq, D_V_DIM, t), F32),
                            pltpu.VMEM((2 * nq, 1, t), F32)]),
        out_shape=jax.ShapeDtypeStruct((bsz * seq, D_WIDTH), BF16),
        compiler_params=pltpu.CompilerParams(dimension_semantics=("arbitrary", "arbitrary"),
                                             vmem_limit_bytes=VMEM_LIMIT),
    )(ii, jj, qd, kd, vd, lam_rows, subw)


def _merge_kernel(x_ref, hm_ref, hd_ref, sgm_ref, sgd_ref, mod_ref, wbm_ref, wbd_ref, wout_ref, n2w_ref,
                  wr_hi_ref, wr_lo_ref, br_ref, x1_ref, h2_ref, comb_ref):
    gate1 = mod_ref[0, 2:3, :]
    shift2 = mod_ref[0, 3:4, :]
    scale2 = mod_ref[0, 4:5, :]
    merged = (sgm_ref[...].astype(F32) * _dot(hm_ref[...], wbm_ref[...])
              + sgd_ref[...].astype(F32) * _dot(hd_ref[...], wbd_ref[...]))
    x1 = x_ref[...] + gate1 * _dot(merged.astype(BF16), wout_ref[...])
    x1_ref[...] = x1
    ms = jnp.mean(x1 * x1, axis=-1, keepdims=True)
    h2 = (x1 * lax.rsqrt(ms + EPS) * n2w_ref[...]) * (1.0 + scale2) + shift2
    h2_hi = h2.astype(BF16)
    h2_ref[...] = h2_hi
    h2_lo = (h2 - h2_hi.astype(F32)).astype(BF16)
    logits = (_dot(h2_hi, wr_hi_ref[...]) + _dot(h2_lo, wr_hi_ref[...]) + _dot(h2_hi, wr_lo_ref[...])
              + br_ref[...])
    lane = lax.broadcasted_iota(jnp.int32, logits.shape, 1)
    lane_f = lane.astype(F32)
    big = float(LANES)

    def top(vals):
        mx = jnp.max(vals, axis=-1, keepdims=True)
        idx = jnp.min(jnp.where(vals == mx, lane_f, big), axis=-1, keepdims=True)
        return mx, idx

    is_group = lane < N_GROUPS
    gmax, gidx = top(jnp.where(is_group, logits, NEG_INF))
    gsum = jnp.sum(jnp.where(is_group, jnp.exp(logits - gmax), 0.0), axis=-1, keepdims=True)
    pg_top = 1.0 / gsum
    lo = N_GROUPS + EXPERTS_PER_GROUP * gidx
    in_group = (lane_f >= lo) & (lane_f < lo + EXPERTS_PER_GROUP)
    el = jnp.where(in_group, logits, NEG_INF)
    e1max, e1idx = top(el)
    e2max, e2idx = top(jnp.where(lane_f == e1idx, NEG_INF, el))
    p2 = jnp.exp(e2max - e1max)
    w1 = pg_top / (1.0 + p2)
    w2 = w1 * p2
    comb_ref[...] = jnp.where(lane_f == e1idx, w1, 0.0) + jnp.where(lane_f == e2idx, w2, 0.0)


def _merge_call(x2, hm, hd, sgm, sgd, mod3, wbm, wbd, wout, n2w, wr_hi, wr_lo, br, seq):
    n, d = x2.shape
    tm = TM_MERGE
    tiles_per_seq = seq // tm
    row = lambda w: pl.BlockSpec((tm, w), lambda i: (i, 0))
    consts = [wbm, wbd, wout, n2w, wr_hi, wr_lo, br]
    return pl.pallas_call(
        _merge_kernel,
        grid=(n // tm,),
        in_specs=[row(d), row(M_WIDTH), row(D_WIDTH), row(d), row(d),
                  pl.BlockSpec((1, 6, d), lambda i: (i // tiles_per_seq, 0, 0))]
                 + [_const_spec(a.shape) for a in consts],
        out_specs=[row(d), row(d), row(LANES)],
        out_shape=[jax.ShapeDtypeStruct((n, d), F32), jax.ShapeDtypeStruct((n, d), BF16),
                   jax.ShapeDtypeStruct((n, LANES), F32)],
        compiler_params=pltpu.CompilerParams(dimension_semantics=("arbitrary",),
                                             vmem_limit_bytes=VMEM_LIMIT),
    )(x2, hm, hd, sgm, sgd, mod3, *consts)


def _moe_kernel(h2_ref, comb_ref, x1_ref, mod_ref, wg_ref, wu_ref, wd_ref, out_ref, acc_ref):
    e = pl.program_id(1)

    @pl.when(e == 0)
    def _():
        acc_ref[...] = jnp.zeros(acc_ref.shape, F32)

    h2 = h2_ref[...]
    gte = _dot(h2, wg_ref[0].astype(BF16))
    up = _dot(h2, wu_ref[0].astype(BF16))
    act = (gte * _sigmoid(gte) * up).astype(BF16)
    lane = lax.broadcasted_iota(jnp.int32, comb_ref.shape, 1)
    ce = jnp.sum(jnp.where(lane == e + N_GROUPS, comb_ref[...], 0.0), axis=-1, keepdims=True)
    acc_ref[...] += ce * _dot(act, wd_ref[0].astype(BF16))

    @pl.when(e == N_EXPERTS - 1)
    def _():
        gate2 = mod_ref[0, 5:6, :]
        out_ref[...] = x1_ref[...] + gate2 * acc_ref[...]


def _moe_call(h2, comb, x1, mod3, w_gate, w_up, w_down, seq):
    n, d = x1.shape
    tm = TM_MOE
    tiles_per_seq = seq // tm
    row = lambda w: pl.BlockSpec((tm, w), lambda i, e: (i, 0))
    return pl.pallas_call(
        _moe_kernel,
        grid=(n // tm, N_EXPERTS),
        in_specs=[row(d), row(LANES), row(d),
                  pl.BlockSpec((1, 6, d), lambda i, e: (i // tiles_per_seq, 0, 0)),
                  pl.BlockSpec((1, d, D_EXPERT), lambda i, e: (e, 0, 0)),
                  pl.BlockSpec((1, d, D_EXPERT), lambda i, e: (e, 0, 0)),
                  pl.BlockSpec((1, D_EXPERT, d), lambda i, e: (e, 0, 0))],
        out_specs=row(d),
        out_shape=jax.ShapeDtypeStruct((n, d), F32),
        scratch_shapes=[pltpu.VMEM((tm, d), F32)],
        compiler_params=pltpu.CompilerParams(dimension_semantics=("arbitrary", "arbitrary"),
                                             vmem_limit_bytes=VMEM_LIMIT),
    )(h2, comb, x1, mod3, w_gate, w_up, w_down)


def _rope_freq_row():
    inv = ROPE_THETA ** (-np.arange(0, ROPE_DIM, 2, dtype=np.float32) / ROPE_DIM)
    row = np.zeros((LANES,), np.float32)
    half = ROPE_DIM // 2
    for base in range(0, LANES, D_HEAD_DIM):
        row[base:base + half] = inv
        row[base + half:base + ROPE_DIM] = inv
    return jnp.asarray(row.reshape(1, LANES))


def _block_diag_ones():
    idx = np.arange(MXU_DIM) // D_HEAD_DIM
    return jnp.asarray((idx[:, None] == idx[None, :]).astype(np.float32), dtype=BF16)


def kernel(x, c, positions, w_ada, b_ada, norm1_w, w_in, b_igate, b_fgate, conv_w, conv_b, mlstm_norm_w, q_norm_w, k_norm_w, lam_q1, lam_k1, lam_q2, lam_k2, subln_w, w_br_m, w_br_d, w_out, norm2_w, w_rg, b_rg, w_re, b_re, w_gate, w_up, w_down):
    bsz, seq, d = x.shape
    n = bsz * seq
    x2 = x.reshape(n, d)
    pos2 = positions.reshape(n, 1)
    l = 0

    mod3 = _ada_call(c, w_ada[l], b_ada[l]).reshape(bsz, 6, d)

    wi = w_in[l]
    o = 0
    segs = {}
    for name, width in (("qk", 2 * M_WIDTH), ("v", M_WIDTH), ("o", M_WIDTH), ("g", 2 * M_HEADS),
                        ("qd", D_WIDTH), ("kd", D_WIDTH), ("vd", D_WIDTH), ("gm", d), ("gd", d)):
        segs[name] = wi[:, o:o + width]
        o += width
    segs["g"] = jnp.pad(segs["g"], ((0, 0), (0, LANES - 2 * M_HEADS)))
    weights = [segs[k].astype(BF16) for k in ("qk", "v", "o", "g", "qd", "kd", "vd", "gm", "gd")]
    gate_bias = jnp.pad(jnp.concatenate([b_igate[l], b_fgate[l]]), (0, LANES - 2 * M_HEADS)).reshape(1, LANES)
    reps = D_WIDTH // D_HEAD_DIM
    smalls = [conv_w[l], conv_b[l].reshape(1, -1), gate_bias,
              jnp.tile(q_norm_w[l], reps).reshape(1, D_WIDTH), jnp.tile(k_norm_w[l], reps).reshape(1, D_WIDTH),
              _rope_freq_row(), _block_diag_ones()]
    qm, km, vm, so, gates, qd, kd, vd, sgm, sgd = _in_call(
        x2, pos2, mod3, norm1_w[l].reshape(1, d), weights, smalls, seq)

    hm = _mlstm_call(qm, km, vm, so, gates, mlstm_norm_w[l].reshape(1, M_WIDTH), bsz, seq)

    lam_rows = jnp.stack([lam_q1[l], lam_k1[l], lam_q2[l], lam_k2[l]])
    hd = _attn_call(qd, kd, vd, lam_rows, subln_w[l].reshape(1, D_V_DIM), bsz, seq)

    w_r = jnp.pad(jnp.concatenate([w_rg[l], w_re[l]], axis=1), ((0, 0), (0, LANES - N_GROUPS - N_EXPERTS)))
    wr_hi = w_r.astype(BF16)
    wr_lo = (w_r - wr_hi.astype(F32)).astype(BF16)
    b_r = jnp.pad(jnp.concatenate([b_rg[l], b_re[l]]), (0, LANES - N_GROUPS - N_EXPERTS)).reshape(1, LANES)
    x1, h2, comb = _merge_call(x2, hm, hd, sgm, sgd, mod3, w_br_m[l].astype(BF16), w_br_d[l].astype(BF16),
                               w_out[l].astype(BF16), norm2_w[l].reshape(1, d), wr_hi, wr_lo, b_r, seq)

    out = _moe_call(h2, comb, x1, mod3, w_gate[l], w_up[l], w_down[l], seq)
    return out.reshape(bsz, seq, d)
```

```python
import functools
import math

import numpy as np
import jax
import jax.numpy as jnp
from jax import lax
from jax.experimental import pallas as pl
from jax.experimental.pallas import tpu as pltpu

F32 = jnp.float32
BF16 = jnp.bfloat16

D_MODEL = 1024
M_HEADS = 4
M_HEAD_DIM = 128
M_WIDTH = M_HEADS * M_HEAD_DIM
CONV_WIDTH = 4
D_HEADS = 4
D_HEAD_DIM = 64
D_V_DIM = 2 * D_HEAD_DIM
D_WIDTH = D_HEADS * D_V_DIM
ROPE_THETA = 500000.0
ROPE_DIM = D_HEAD_DIM // 4
N_GROUPS = 4
EXPERTS_PER_GROUP = 8
N_EXPERTS = N_GROUPS * EXPERTS_PER_GROUP
D_EXPERT = D_MODEL // 4
EPS = 1e-6
LAM_INIT = 0.8 - 0.6 * math.exp(-0.3 * 0)

LANES = 128
SUBLANES = 8
MXU_DIM = 256
VMEM_LIMIT = 56 * 1024 * 1024

TM_IN = 512
TM_MERGE = 512
M_CHUNK = 256
M_GROUP = 2
A_BLOCK = 512
LOG2_E = math.log2(math.e)
A_UNROLL = 4
BOUND_SLACK = 1.01
MOE_R = 256
MOE_RS = MOE_R + SUBLANES
MOE_U = 16
NEG_INF = float("-inf")


def _dot(a, b):
    return jnp.dot(a, b, preferred_element_type=F32)


def _dot_nt(a, b):
    return lax.dot_general(a, b, (((1,), (1,)), ((), ())), preferred_element_type=F32)


def _dot_tn(a, b):
    return lax.dot_general(a, b, (((0,), (0,)), ((), ())), preferred_element_type=F32)


def _sigmoid(v):
    return 1.0 / (1.0 + jnp.exp(-v))


def _const_spec(shape):
    nd = len(shape)
    return pl.BlockSpec(shape, lambda *_: (0,) * nd)


def _ada_kernel(c_ref, w_ref, b_ref, o_ref):
    o_ref[...] = _dot(c_ref[...].astype(BF16), w_ref[...].astype(BF16)) + b_ref[...]


def _ada_call(c, w_ada, b_ada):
    bsz, d = c.shape
    cols = w_ada.shape[1]
    tn = 1024
    return pl.pallas_call(
        _ada_kernel,
        grid=(cols // tn,),
        in_specs=[pl.BlockSpec((bsz, d), lambda j: (0, 0)),
                  pl.BlockSpec((d, tn), lambda j: (0, j)),
                  pl.BlockSpec((1, tn), lambda j: (0, j))],
        out_specs=pl.BlockSpec((bsz, tn), lambda j: (0, j)),
        out_shape=jax.ShapeDtypeStruct((bsz, cols), F32),
        compiler_params=pltpu.CompilerParams(dimension_semantics=("arbitrary",)),
    )(c, w_ada, b_ada.reshape(1, cols))


def _in_kernel(tiles_per_seq,
               x_ref, pos_ref, mod_ref, n1w_ref,
               wqk_ref, wv_ref, wo_ref, wg_ref, wqd_ref, wkd_ref, wvd_ref, wgm_ref, wgd_ref,
               cw_ref, cb_ref, gb_ref, qnw_ref, knw_ref, freq_ref, bd_ref,
               qm_ref, km_ref, vm_ref, so_ref, gates_ref, qd_ref, kd_ref, vd_ref, sgm_ref, sgd_ref,
               cbuf):
    tm = x_ref.shape[0]
    i = pl.program_id(0)

    x = x_ref[...]
    ms = jnp.mean(x * x, axis=-1, keepdims=True)
    scale1 = mod_ref[0, 1:2, :]
    shift1 = mod_ref[0, 0:1, :]
    h = (x * lax.rsqrt(ms + EPS) * n1w_ref[...]) * (1.0 + scale1) + shift1
    hb = h.astype(BF16)

    y = _dot(hb, wqk_ref[...])

    @pl.when(i % tiles_per_seq == 0)
    def _():
        cbuf[0:SUBLANES, :] = jnp.zeros((SUBLANES, 2 * M_WIDTH), F32)

    cbuf[SUBLANES:SUBLANES + tm, :] = y
    acc = cb_ref[...] + cw_ref[3:4, :] * y
    for d in range(1, CONV_WIDTH):
        acc = acc + cw_ref[3 - d:4 - d, :] * cbuf[SUBLANES - d:SUBLANES - d + tm, :]
    cbuf[0:SUBLANES, :] = cbuf[tm:tm + SUBLANES, :]
    a = acc * _sigmoid(acc)
    qm_ref[...] = a[:, :M_WIDTH].astype(BF16)
    km_ref[...] = (a[:, M_WIDTH:] * (M_HEAD_DIM ** -0.5)).astype(BF16)

    vm_ref[...] = _dot(hb, wv_ref[...]).astype(BF16)
    so_ref[...] = _sigmoid(_dot(hb, wo_ref[...])).astype(BF16)

    g = _dot(hb, wg_ref[...]) + gb_ref[...]
    lane = lax.broadcasted_iota(jnp.int32, g.shape, 1)
    logsig = jnp.minimum(g, 0.0) - jnp.log(1.0 + jnp.exp(-jnp.abs(g)))
    gates_ref[...] = jnp.where(lane < M_HEADS, g, logsig)

    ang = pos_ref[...].astype(F32) * freq_ref[...]
    cos = jnp.cos(ang)
    sin = jnp.sin(ang)
    lane_in_map = lax.broadcasted_iota(jnp.int32, ang.shape, 1) % D_HEAD_DIM
    half = ROPE_DIM // 2
    s_from_lo = jnp.where(lane_in_map >= half, sin, 0.0)
    s_from_hi = jnp.where(lane_in_map < half, -sin, 0.0)
    reps = D_WIDTH // LANES
    cos_w = jnp.concatenate([cos] * reps, axis=1)
    slo_w = jnp.concatenate([s_from_lo] * reps, axis=1)
    shi_w = jnp.concatenate([s_from_hi] * reps, axis=1)

    def qk_norm_rope(w_ref, nw_ref, out_scale):
        t = _dot(hb, w_ref[...])
        sq = (t * t).astype(BF16)
        gs = [_dot(sq[:, c * MXU_DIM:(c + 1) * MXU_DIM], bd_ref[...]) for c in range(D_WIDTH // MXU_DIM)]
        msq = jnp.concatenate(gs, axis=1) * (1.0 / D_HEAD_DIM)
        tn = t * lax.rsqrt(msq + EPS) * nw_ref[...]
        rot = (tn * cos_w + pltpu.roll(tn, half, 1) * slo_w
               + pltpu.roll(tn, D_WIDTH - half, 1) * shi_w)
        return (rot * out_scale).astype(BF16)

    qd_ref[...] = qk_norm_rope(wqd_ref, qnw_ref, D_HEAD_DIM ** -0.5 * LOG2_E)
    kd_ref[...] = qk_norm_rope(wkd_ref, knw_ref, 1.0)
    vd_ref[...] = _dot(hb, wvd_ref[...]).astype(BF16)
    sgm_ref[...] = _sigmoid(_dot(hb, wgm_ref[...])).astype(BF16)
    sgd_ref[...] = _sigmoid(_dot(hb, wgd_ref[...])).astype(BF16)


def _in_call(x2, pos2, mod3, n1w, weights, smalls, seq):
    n, d = x2.shape
    tm = TM_IN
    tiles_per_seq = seq // tm
    row = lambda w: pl.BlockSpec((tm, w), lambda i: (i, 0))
    in_specs = [row(d), row(1),
                pl.BlockSpec((1, 6, d), lambda i: (i // tiles_per_seq, 0, 0)),
                _const_spec(n1w.shape)]
    in_specs += [_const_spec(w.shape) for w in weights]
    in_specs += [_const_spec(s.shape) for s in smalls]
    out_widths = [(M_WIDTH, BF16), (M_WIDTH, BF16), (M_WIDTH, BF16), (M_WIDTH, BF16), (LANES, F32),
                  (D_WIDTH, BF16), (D_WIDTH, BF16), (D_WIDTH, BF16), (d, BF16), (d, BF16)]
    return pl.pallas_call(
        functools.partial(_in_kernel, tiles_per_seq),
        grid=(n // tm,),
        in_specs=in_specs,
        out_specs=[row(w) for w, _ in out_widths],
        out_shape=[jax.ShapeDtypeStruct((n, w), dt) for w, dt in out_widths],
        scratch_shapes=[pltpu.VMEM((tm + 2 * SUBLANES, 2 * M_WIDTH), F32)],
        compiler_params=pltpu.CompilerParams(dimension_semantics=("arbitrary",),
                                             vmem_limit_bytes=VMEM_LIMIT),
    )(x2, pos2, mod3, n1w, *weights, *smalls)


def _mlstm_kernel(grp, q_ref, k_ref, v_ref, so_ref, gates_ref, nw_ref, out_ref, cx_ref, m_ref):
    @pl.when(pl.program_id(1) == 0)
    def _():
        cx_ref[...] = jnp.zeros(cx_ref.shape, F32)
        m_ref[...] = jnp.zeros(m_ref.shape, F32)

    for gi in range(grp):
        _mlstm_chunk(q_ref.at[gi], k_ref.at[gi], v_ref.at[gi], so_ref.at[gi], gates_ref.at[gi], nw_ref,
                     out_ref.at[gi], cx_ref.at[gi], m_ref.at[gi])


def _mlstm_chunk(q_ref, k_ref, v_ref, so_ref, gates_ref, nw_ref, out_ref, cx_ref, m_ref):
    L = q_ref.shape[0]
    hd = M_HEAD_DIM
    g = gates_ref[...]
    r_i = lax.broadcasted_iota(jnp.int32, (L, L), 0)
    c_i = lax.broadcasted_iota(jnp.int32, (L, L), 1)
    causal = c_i <= r_i
    bc = jnp.dot(causal.astype(F32), g, preferred_element_type=F32,
                 precision=lax.Precision.HIGHEST)
    g_t = g.T
    bc_t = bc.T
    ones_col = jnp.where(lax.broadcasted_iota(jnp.int32, (L, hd), 1) == 0, 1.0, 0.0).astype(BF16)

    for h in range(M_HEADS):
        sl = slice(h * hd, (h + 1) * hd)
        i_col = g[:, h:h + 1]
        b_col = bc[:, M_HEADS + h:M_HEADS + h + 1]
        i_row = g_t[h:h + 1, :]
        b_row = bc_t[M_HEADS + h:M_HEADS + h + 1, :]
        m_prev = m_ref[h][0:1, 0:1]
        cx = cx_ref[h]

        dmat = jnp.where(causal, b_col + (i_row - b_row), NEG_INF)
        m_inter = b_col + m_prev
        m_t = jnp.maximum(m_inter, jnp.max(dmat, axis=1, keepdims=True))
        q = q_ref[:, sl]
        k = k_ref[:, sl]
        vx = jnp.concatenate([v_ref[:, sl], ones_col], axis=1)
        s = _dot_nt(q, k) * jnp.exp(dmat - m_t)
        inter = jnp.exp(m_inter - m_t)
        numx = _dot(s.astype(BF16), vx) + inter * _dot(q, cx.astype(BF16))
        den = jnp.maximum(jnp.abs(numx[:, hd:hd + 1]), jnp.exp(-m_t))
        hh = numx[:, :hd] * (1.0 / den)

        b_last = b_col[L - 1:L, :]
        w_col = b_last + (i_col - b_col)
        m_new = jnp.maximum(b_last + m_prev, jnp.max(w_col, axis=0, keepdims=True))
        wexp = jnp.exp(w_col - m_new)
        decay = jnp.exp(b_last + m_prev - m_new)
        kw = (k.astype(F32) * wexp).astype(BF16)
        cx_ref[h] = decay * cx + _dot_tn(kw, vx)
        m_ref[h] = jnp.broadcast_to(m_new, m_ref.shape[1:])

        hn = hh * lax.rsqrt(jnp.mean(hh * hh, axis=-1, keepdims=True) + EPS) * nw_ref[:, sl]
        out_ref[:, sl] = (hn * so_ref[:, sl].astype(F32)).astype(BF16)


def _mlstm_call(qm, km, vm, so, gates, nw, bsz, seq):
    L = M_CHUNK
    grp = math.gcd(bsz, M_GROUP)
    blk = lambda w: pl.BlockSpec((grp, L, w), lambda b, c: (b, c, 0))
    seqs = lambda a: a.reshape(bsz, seq, a.shape[-1])
    out = pl.pallas_call(
        functools.partial(_mlstm_kernel, grp),
        grid=(bsz // grp, seq // L),
        in_specs=[blk(M_WIDTH), blk(M_WIDTH), blk(M_WIDTH), blk(M_WIDTH), blk(LANES),
                  pl.BlockSpec((1, M_WIDTH), lambda b, c: (0, 0))],
        out_specs=blk(M_WIDTH),
        out_shape=jax.ShapeDtypeStruct((bsz, seq, M_WIDTH), BF16),
        scratch_shapes=[pltpu.VMEM((grp, M_HEADS, M_HEAD_DIM, 2 * M_HEAD_DIM), F32),
                        pltpu.VMEM((grp, M_HEADS, SUBLANES, LANES), F32)],
        compiler_params=pltpu.CompilerParams(dimension_semantics=("arbitrary", "arbitrary"),
                                             vmem_limit_bytes=VMEM_LIMIT),
    )(seqs(qm), seqs(km), seqs(vm), seqs(so), seqs(gates), nw)
    return out.reshape(bsz * seq, M_WIDTH)


def _attn_kernel(ii_ref, jj_ref, q_ref, k_ref, v_ref, lam_ref, sw_ref, out_ref,
                 vt_ref, kp_ref, qa_ref, acc_ref, l_ref):
    seq = q_ref.shape[0]
    t = A_BLOCK
    nq = seq // t
    lp = lam_ref[...]
    lam = (jnp.exp(jnp.sum(lp[0:1] * lp[1:2], axis=-1, keepdims=True))
           - jnp.exp(jnp.sum(lp[2:3] * lp[3:4], axis=-1, keepdims=True)) + LAM_INIT)
    for jb in range(nq):
        vt_ref[jb] = v_ref[jb * t:(jb + 1) * t, :].T

    aux = (D_HEAD_DIM, 0)
    k = k_ref[...]
    ksq = jnp.square(k.astype(F32))
    lane_s = lax.broadcasted_iota(jnp.int32, (seq, LANES), 1)
    in_map_s = (lane_s < D_HEAD_DIM, lane_s >= D_HEAD_DIM)
    kmax2 = []
    for mp in range(2):
        kn2 = jnp.sum(jnp.where(in_map_s[mp], ksq, 0.0), axis=1, keepdims=True)
        kmax2.append(jnp.max(kn2, axis=0, keepdims=True))
        kp_ref[mp] = jnp.where(in_map_s[mp], k, jnp.where(lane_s == aux[mp], 1.0, 0.0).astype(BF16))

    q = q_ref[...].astype(F32)
    qsq = q * q
    for mp in range(2):
        qn2 = jnp.sum(jnp.where(in_map_s[mp], qsq, 0.0), axis=1, keepdims=True)
        bound = jnp.sqrt(qn2 * kmax2[mp]) * BOUND_SLACK
        qa_ref[mp] = jnp.where(in_map_s[mp], q, jnp.where(lane_s == aux[mp], -bound, 0.0)).astype(BF16)
    l_ref[...] = jnp.zeros(l_ref.shape, F32)
    acc_ref[...] = jnp.zeros(acc_ref.shape, F32)

    k_i = lax.broadcasted_iota(jnp.int32, (t, t), 0)
    q_i = lax.broadcasted_iota(jnp.int32, (t, t), 1)
    diag_mask = k_i <= q_i

    def step(i, j, masked):
        koff = pl.multiple_of(j * t, t)
        qoff = pl.multiple_of(i * t, t)
        vt = vt_ref[j]
        for mp in range(2):
            s = _dot_nt(kp_ref[mp, pl.ds(koff, t), :], qa_ref[mp, pl.ds(qoff, t), :])
            if masked:
                s = jnp.where(diag_mask, s, NEG_INF)
            p = jnp.exp2(s)
            l_ref[mp * nq + i] += jnp.sum(p, axis=0, keepdims=True)
            acc_ref[mp * nq + i] += _dot(vt, p.astype(BF16))

    def run(n_steps, step_at):
        trips = n_steps // A_UNROLL

        def body(u, c):
            for r in range(A_UNROLL):
                step_at(u * A_UNROLL + r)
            return c

        if trips:
            lax.fori_loop(0, trips, body, 0)
        for p in range(trips * A_UNROLL, n_steps):
            step_at(p)

    run(nq * (nq - 1) // 2, lambda p: step(ii_ref[p], jj_ref[p], False))
    run(nq, lambda p: step(p, p, True))

    def finish(i, c):
        o_t = (acc_ref[i] * (1.0 / l_ref[i]) - lam * (acc_ref[nq + i] * (1.0 / l_ref[nq + i])))
        nrm = lax.rsqrt(jnp.mean(o_t * o_t, axis=0, keepdims=True) + EPS)
        on = (o_t * nrm).T * sw_ref[...] * (1.0 - LAM_INIT)
        out_ref[pl.ds(pl.multiple_of(i * t, t), t), :] = on.astype(BF16)
        return c

    lax.fori_loop(0, nq, finish, 0)


def _attn_call(qd, kd, vd, lam_rows, subw, bsz, seq):
    t = A_BLOCK
    nq = seq // t
    pairs = [(i, j) for i in range(nq) for j in range(i)]
    ii = jnp.asarray([p[0] for p in pairs] or [0], jnp.int32)
    jj = jnp.asarray([p[1] for p in pairs] or [0], jnp.int32)
    blk = pl.BlockSpec((seq, LANES), lambda b, h, *_: (b, h))
    const = lambda shape: pl.BlockSpec(shape, lambda b, h, *_: (0,) * len(shape))
    return pl.pallas_call(
        _attn_kernel,
        grid_spec=pltpu.PrefetchScalarGridSpec(
            num_scalar_prefetch=2,
            grid=(bsz, D_HEADS),
            in_specs=[blk, blk, blk, const(lam_rows.shape), const(subw.shape)],
            out_specs=blk,
            scratch_shapes=[pltpu.VMEM((nq, D_V_DIM, t), BF16),
                            pltpu.VMEM((2, seq, LANES), BF16),
                            pltpu.VMEM((2, seq, LANES), BF16),
                            pltpu.VMEM((2 * nq, D_V_DIM, t), F32),
                            pltpu.VMEM((2 * nq, 1, t), F32)]),
        out_shape=jax.ShapeDtypeStruct((bsz * seq, D_WIDTH), BF16),
        compiler_params=pltpu.CompilerParams(dimension_semantics=("arbitrary", "arbitrary"),
                                             vmem_limit_bytes=VMEM_LIMIT),
    )(ii, jj, qd, kd, vd, lam_rows, subw)


def _merge_kernel(tiles_per_seq, x_ref, hm_ref, hd_ref, sgm_ref, sgd_ref, mod_ref, wbm_ref, wbd_ref, wout_ref,
                  n2w_ref, wr_hi_ref, wr_lo_ref, br_ref, x1_ref, h2_ref, route_ref, cnt_ref, carry_ref):
    gate1 = mod_ref[0, 2:3, :]
    shift2 = mod_ref[0, 3:4, :]
    scale2 = mod_ref[0, 4:5, :]
    merged = (sgm_ref[...].astype(F32) * _dot(hm_ref[...], wbm_ref[...])
              + sgd_ref[...].astype(F32) * _dot(hd_ref[...], wbd_ref[...]))
    x1 = x_ref[...] + gate1 * _dot(merged.astype(BF16), wout_ref[...])
    x1_ref[...] = x1
    ms = jnp.mean(x1 * x1, axis=-1, keepdims=True)
    h2 = (x1 * lax.rsqrt(ms + EPS) * n2w_ref[...]) * (1.0 + scale2) + shift2
    h2_ref[...] = h2
    h2_hi = h2.astype(BF16)
    h2_lo = (h2 - h2_hi.astype(F32)).astype(BF16)
    logits = (_dot(h2_hi, wr_hi_ref[...]) + _dot(h2_lo, wr_hi_ref[...]) + _dot(h2_hi, wr_lo_ref[...])
              + br_ref[...])
    lane = lax.broadcasted_iota(jnp.int32, logits.shape, 1)
    lane_f = lane.astype(F32)
    big = float(LANES)

    def top(vals):
        mx = jnp.max(vals, axis=-1, keepdims=True)
        idx = jnp.min(jnp.where(vals == mx, lane_f, big), axis=-1, keepdims=True)
        return mx, idx

    is_group = lane < N_GROUPS
    gmax, gidx = top(jnp.where(is_group, logits, NEG_INF))
    gsum = jnp.sum(jnp.where(is_group, jnp.exp(logits - gmax), 0.0), axis=-1, keepdims=True)
    pg_top = 1.0 / gsum
    lo = N_GROUPS + EXPERTS_PER_GROUP * gidx
    in_group = (lane_f >= lo) & (lane_f < lo + EXPERTS_PER_GROUP)
    el = jnp.where(in_group, logits, NEG_INF)
    e1max, e1idx = top(el)
    e2max, e2idx = top(jnp.where(lane_f == e1idx, NEG_INF, el))
    p2 = jnp.exp(e2max - e1max)
    w1 = pg_top / (1.0 + p2)
    w2 = w1 * p2

    @pl.when(pl.program_id(0) % tiles_per_seq == 0)
    def _():
        carry_ref[...] = jnp.zeros(carry_ref.shape, F32)

    tm = logits.shape[0]
    is_e1 = lane_f == e1idx
    is_e2 = lane_f == e2idx
    onehot = jnp.where(is_e1 | is_e2, 1.0, 0.0)
    r_i = lax.broadcasted_iota(jnp.int32, (tm, tm), 0)
    c_i = lax.broadcasted_iota(jnp.int32, (tm, tm), 1)
    before = jnp.where(c_i < r_i, 1.0, 0.0).astype(BF16)
    cum = _dot(before, onehot.astype(BF16)) + carry_ref[...]
    r1 = jnp.sum(jnp.where(is_e1, cum, 0.0), axis=-1, keepdims=True)
    r2 = jnp.sum(jnp.where(is_e2, cum, 0.0), axis=-1, keepdims=True)
    carry = carry_ref[...] + jnp.sum(onehot, axis=0, keepdims=True)
    carry_ref[...] = carry
    cnt_ref[0] = carry

    ints = jnp.where(lane == 0, e1idx - N_GROUPS,
                     jnp.where(lane == 1, e2idx - N_GROUPS, jnp.where(lane == 2, r1, r2))).astype(jnp.int32)
    wbits = pltpu.bitcast(jnp.where(lane == 4, w1, jnp.where(lane == 5, w2, 0.0)), jnp.int32)
    route_ref[...] = jnp.where(lane < 4, ints, wbits)


def _merge_call(x2, hm, hd, sgm, sgd, mod3, wbm, wbd, wout, n2w, wr_hi, wr_lo, br, seq):
    n, d = x2.shape
    tm = TM_MERGE
    tiles_per_seq = seq // tm
    row = lambda w: pl.BlockSpec((tm, w), lambda i: (i, 0))
    consts = [wbm, wbd, wout, n2w, wr_hi, wr_lo, br]
    return pl.pallas_call(
        functools.partial(_merge_kernel, tiles_per_seq),
        grid=(n // tm,),
        in_specs=[row(d), row(M_WIDTH), row(D_WIDTH), row(d), row(d),
                  pl.BlockSpec((1, 6, d), lambda i: (i // tiles_per_seq, 0, 0))]
                 + [_const_spec(a.shape) for a in consts],
        out_specs=[row(d), row(d), row(LANES), pl.BlockSpec((1, 1, LANES), lambda i: (i, 0, 0))],
        out_shape=[jax.ShapeDtypeStruct((n, d), F32), jax.ShapeDtypeStruct((n, d), F32),
                   jax.ShapeDtypeStruct((n, LANES), jnp.int32),
                   jax.ShapeDtypeStruct((n // tm, 1, LANES), F32)],
        scratch_shapes=[pltpu.VMEM((1, LANES), F32)],
        compiler_params=pltpu.CompilerParams(dimension_semantics=("arbitrary",),
                                             vmem_limit_bytes=VMEM_LIMIT),
    )(x2, hm, hd, sgm, sgd, mod3, *consts)


def _moe_kernel(dest_hbm, rw_hbm, meta_hbm, h2_hbm, x1_hbm, mod_ref, wgu_hbm, wd_hbm, out_hbm,
                dest, rw, meta, inv, h2buf, acc, wgu_buf, wd_buf, xst, yst, sems):
    b = pl.program_id(0)
    t_blk = h2buf.shape[0] - 1
    spare = 2 * t_blk
    r = MOE_R

    def weight_copies(e, slot):
        return (pltpu.make_async_copy(wgu_hbm.at[e], wgu_buf.at[slot], sems.at[5 + slot]),
                pltpu.make_async_copy(wd_hbm.at[e], wd_buf.at[slot], sems.at[7 + slot]))

    route_copies = (pltpu.make_async_copy(dest_hbm.at[b], dest, sems.at[0]),
                    pltpu.make_async_copy(rw_hbm.at[b], rw, sems.at[1]),
                    pltpu.make_async_copy(meta_hbm.at[b], meta, sems.at[2]))
    h2_copy = pltpu.make_async_copy(h2_hbm.at[b], h2buf.at[pl.ds(0, t_blk)], sems.at[3])
    x1_copy = pltpu.make_async_copy(x1_hbm.at[b], acc.at[pl.ds(0, t_blk)], sems.at[4])
    out_copy = pltpu.make_async_copy(acc.at[pl.ds(0, t_blk)], out_hbm.at[b], sems.at[9])

    for cp in route_copies + (h2_copy, x1_copy) + weight_copies(0, 0):
        cp.start()
    xst[...] = jnp.zeros(xst.shape, F32)
    yst[...] = jnp.zeros(yst.shape, F32)
    h2buf[t_blk] = jnp.zeros(h2buf.shape[1:], F32)
    acc[t_blk] = jnp.zeros(acc.shape[1:], F32)
    for cp in route_copies:
        cp.wait()

    dump = inv.shape[0] - 1

    def pad_body(e, c):
        end = meta[N_EXPERTS + e + 1]
        for j in range(MOE_U - 1):
            pos = meta[N_EXPERTS + e] + meta[e] + j
            inv[jnp.where(pos < end, pos, dump)] = spare
        return c

    lax.fori_loop(0, N_EXPERTS, pad_body, 0)

    def inv_body(u, c):
        for j in range(MOE_U):
            s = u * MOE_U + j
            inv[dest[s]] = s
        return c

    lax.fori_loop(0, spare // MOE_U, inv_body, 0)
    h2_copy.wait()
    x1_copy.wait()
    gate2 = mod_ref[0, 5:6, :]
    n_chunks = D_MODEL // LANES

    def expert_body(e, c):
        slot = e % 2
        for cp in weight_copies(e, slot):
            cp.wait()

        @pl.when(e + 1 < N_EXPERTS)
        def _():
            for cp in weight_copies(e + 1, 1 - slot):
                cp.start()

        seg = meta[N_EXPERTS + e]
        n = meta[e]

        def tile_body(k, c2):
            base = seg + k * r
            groups = (jnp.minimum(n - k * r, r) + MOE_U - 1) // MOE_U

            def gather(u, c3):
                row0 = u * MOE_U
                pos0 = base + row0
                for j in range(MOE_U):
                    xst[pl.ds(row0 + j, SUBLANES, stride=MOE_RS), :] = h2buf[inv[pos0 + j] >> 1]
                return c3

            lax.fori_loop(0, groups, gather, 0)
            x = jnp.concatenate([xst[ch * MOE_RS:ch * MOE_RS + r, :] for ch in range(n_chunks)], axis=1)
            gu = _dot(x.astype(BF16), wgu_buf[slot])
            g = gu[:, :D_EXPERT]
            act = (g * _sigmoid(g) * gu[:, D_EXPERT:]).astype(BF16)
            y = _dot(act, wd_buf[slot]) * gate2
            for ch in range(n_chunks):
                yst[ch * MOE_RS:ch * MOE_RS + r, :] = y[:, ch * LANES:(ch + 1) * LANES]

            def scatter(u, c3):
                toks, vals = [], []
                row0 = u * MOE_U
                pos0 = base + row0
                for j in range(MOE_U):
                    v = inv[pos0 + j]
                    toks.append(v >> 1)
                    vals.append(acc[v >> 1] + rw[v] * yst[pl.ds(row0 + j, SUBLANES, stride=MOE_RS), :])
                for tok, val in zip(toks, vals):
                    acc[tok] = val
                return c3

            lax.fori_loop(0, groups, scatter, 0)
            return c2

        lax.fori_loop(0, (n + r - 1) // r, tile_body, 0)
        return c

    lax.fori_loop(0, N_EXPERTS, expert_body, 0)
    out_copy.start()
    out_copy.wait()


def _moe_call(dest, rw, meta, h2, x1, mod3, wgu, wd):
    nb, t_blk = h2.shape[:2]
    d = D_MODEL
    n_pos = 2 * t_blk + N_EXPERTS * MOE_U + 1
    any_spec = pl.BlockSpec(memory_space=pl.ANY)
    return pl.pallas_call(
        _moe_kernel,
        grid=(nb,),
        in_specs=[any_spec, any_spec, any_spec, any_spec, any_spec,
                  pl.BlockSpec((1, 6, d), lambda i: (i, 0, 0)), any_spec, any_spec],
        out_specs=any_spec,
        out_shape=jax.ShapeDtypeStruct(h2.shape, F32),
        scratch_shapes=[pltpu.SMEM(dest.shape[1:], jnp.int32), pltpu.SMEM(rw.shape[1:], F32),
                        pltpu.SMEM(meta.shape[1:], jnp.int32), pltpu.SMEM((n_pos,), jnp.int32),
                        pltpu.VMEM((t_blk + 1, SUBLANES, LANES), F32),
                        pltpu.VMEM((t_blk + 1, SUBLANES, LANES), F32),
                        pltpu.VMEM((2, d, 2 * D_EXPERT), BF16), pltpu.VMEM((2, D_EXPERT, d), BF16),
                        pltpu.VMEM((SUBLANES * MOE_RS, LANES), F32), pltpu.VMEM((SUBLANES * MOE_RS, LANES), F32),
                        pltpu.SemaphoreType.DMA((10,))],
        compiler_params=pltpu.CompilerParams(dimension_semantics=("arbitrary",),
                                             vmem_limit_bytes=VMEM_LIMIT),
    )(dest, rw, meta, h2, x1, mod3, wgu, wd)


def _rope_freq_row():
    inv = ROPE_THETA ** (-np.arange(0, ROPE_DIM, 2, dtype=np.float32) / ROPE_DIM)
    row = np.zeros((LANES,), np.float32)
    half = ROPE_DIM // 2
    for base in range(0, LANES, D_HEAD_DIM):
        row[base:base + half] = inv
        row[base + half:base + ROPE_DIM] = inv
    return jnp.asarray(row.reshape(1, LANES))


def _block_diag_ones():
    idx = np.arange(MXU_DIM) // D_HEAD_DIM
    return jnp.asarray((idx[:, None] == idx[None, :]).astype(np.float32), dtype=BF16)


def kernel(x, c, positions, w_ada, b_ada, norm1_w, w_in, b_igate, b_fgate, conv_w, conv_b, mlstm_norm_w, q_norm_w, k_norm_w, lam_q1, lam_k1, lam_q2, lam_k2, subln_w, w_br_m, w_br_d, w_out, norm2_w, w_rg, b_rg, w_re, b_re, w_gate, w_up, w_down):
    bsz, seq, d = x.shape
    n = bsz * seq
    x2 = x.reshape(n, d)
    pos2 = positions.reshape(n, 1)
    l = 0

    mod3 = _ada_call(c, w_ada[l], b_ada[l]).reshape(bsz, 6, d)

    wi = w_in[l]
    o = 0
    segs = {}
    for name, width in (("qk", 2 * M_WIDTH), ("v", M_WIDTH), ("o", M_WIDTH), ("g", 2 * M_HEADS),
                        ("qd", D_WIDTH), ("kd", D_WIDTH), ("vd", D_WIDTH), ("gm", d), ("gd", d)):
        segs[name] = wi[:, o:o + width]
        o += width
    segs["g"] = jnp.pad(segs["g"], ((0, 0), (0, LANES - 2 * M_HEADS)))
    weights = [segs[k].astype(BF16) for k in ("qk", "v", "o", "g", "qd", "kd", "vd", "gm", "gd")]
    gate_bias = jnp.pad(jnp.concatenate([b_igate[l], b_fgate[l]]), (0, LANES - 2 * M_HEADS)).reshape(1, LANES)
    reps = D_WIDTH // D_HEAD_DIM
    smalls = [conv_w[l], conv_b[l].reshape(1, -1), gate_bias,
              jnp.tile(q_norm_w[l], reps).reshape(1, D_WIDTH), jnp.tile(k_norm_w[l], reps).reshape(1, D_WIDTH),
              _rope_freq_row(), _block_diag_ones()]
    qm, km, vm, so, gates, qd, kd, vd, sgm, sgd = _in_call(
        x2, pos2, mod3, norm1_w[l].reshape(1, d), weights, smalls, seq)

    hm = _mlstm_call(qm, km, vm, so, gates, mlstm_norm_w[l].reshape(1, M_WIDTH), bsz, seq)

    lam_rows = jnp.stack([lam_q1[l], lam_k1[l], lam_q2[l], lam_k2[l]])
    hd = _attn_call(qd, kd, vd, lam_rows, subln_w[l].reshape(1, D_V_DIM), bsz, seq)

    w_r = jnp.pad(jnp.concatenate([w_rg[l], w_re[l]], axis=1), ((0, 0), (0, LANES - N_GROUPS - N_EXPERTS)))
    wr_hi = w_r.astype(BF16)
    wr_lo = (w_r - wr_hi.astype(F32)).astype(BF16)
    b_r = jnp.pad(jnp.concatenate([b_rg[l], b_re[l]]), (0, LANES - N_GROUPS - N_EXPERTS)).reshape(1, LANES)
    x1, h2, route, cnt = _merge_call(x2, hm, hd, sgm, sgd, mod3, w_br_m[l].astype(BF16), w_br_d[l].astype(BF16),
                                     w_out[l].astype(BF16), norm2_w[l].reshape(1, d), wr_hi, wr_lo, b_r, seq)

    tiles_per_seq = seq // TM_MERGE
    cnt = cnt[tiles_per_seq - 1::tiles_per_seq, 0, N_GROUPS:N_GROUPS + N_EXPERTS].astype(jnp.int32)
    padded = (cnt + MOE_U - 1) // MOE_U * MOE_U
    starts = jnp.cumsum(padded, axis=1) - padded
    meta = jnp.concatenate([cnt, starts, jnp.sum(padded, axis=1, keepdims=True)], axis=1)
    meta = jnp.pad(meta, ((0, 0), (0, LANES - meta.shape[1])))
    experts = route[:, 0:2].reshape(bsz, 2 * seq)
    ranks = route[:, 2:4].reshape(bsz, 2 * seq)
    onehot = experts[:, :, None] == jnp.arange(N_EXPERTS, dtype=jnp.int32)
    dest = ranks + jnp.sum(jnp.where(onehot, starts[:, None, :], 0), axis=-1)
    rw = lax.bitcast_convert_type(route[:, 4:6], F32).reshape(bsz, 2 * seq)
    rw = jnp.pad(rw, ((0, 0), (0, LANES)))
    wgu = jnp.concatenate([w_gate[l], w_up[l]], axis=-1).astype(BF16)
    tok_tiles = (bsz, seq, SUBLANES, LANES)
    out = _moe_call(dest, rw, meta, h2.reshape(tok_tiles), x1.reshape(tok_tiles), mod3, wgu,
                    w_down[l].astype(BF16))
    return out.reshape(bsz, seq, d)
```

```python
import functools
import math

import numpy as np
import jax
import jax.numpy as jnp
from jax import lax
from jax.experimental import pallas as pl
from jax.experimental.pallas import tpu as pltpu

F32 = jnp.float32
BF16 = jnp.bfloat16

D_MODEL = 1024
M_HEADS = 4
M_HEAD_DIM = 128
M_WIDTH = M_HEADS * M_HEAD_DIM
CONV_WIDTH = 4
D_HEADS = 4
D_HEAD_DIM = 64
D_V_DIM = 2 * D_HEAD_DIM
D_WIDTH = D_HEADS * D_V_DIM
ROPE_THETA = 500000.0
ROPE_DIM = D_HEAD_DIM // 4
N_GROUPS = 4
EXPERTS_PER_GROUP = 8
N_EXPERTS = N_GROUPS * EXPERTS_PER_GROUP
D_EXPERT = D_MODEL // 4
EPS = 1e-6
LAM_INIT = 0.8 - 0.6 * math.exp(-0.3 * 0)

LANES = 128
SUBLANES = 8
MXU_DIM = 256
VMEM_LIMIT = 56 * 1024 * 1024

TM_IN = 512
TM_MERGE = 512
M_CHUNK = 256
M_GROUP = 2
A_BLOCK = 512
LOG2_E = math.log2(math.e)
A_UNROLL = 4
BOUND_SLACK = 1.01
MOE_R = 256
MOE_RS = MOE_R + SUBLANES
MOE_U = 16
MOE_OUT_ROWS = 512
NEG_INF = float("-inf")


def _dot(a, b):
    return jnp.dot(a, b, preferred_element_type=F32)


def _dot_nt(a, b):
    return lax.dot_general(a, b, (((1,), (1,)), ((), ())), preferred_element_type=F32)


def _dot_tn(a, b):
    return lax.dot_general(a, b, (((0,), (0,)), ((), ())), preferred_element_type=F32)


def _sigmoid(v):
    return 1.0 / (1.0 + jnp.exp(-v))


def _const_spec(shape):
    nd = len(shape)
    return pl.BlockSpec(shape, lambda *_: (0,) * nd)


def _ada_kernel(c_ref, w_ref, b_ref, o_ref):
    o_ref[...] = _dot(c_ref[...].astype(BF16), w_ref[...].astype(BF16)) + b_ref[...]


def _ada_call(c, w_ada, b_ada):
    bsz, d = c.shape
    cols = w_ada.shape[1]
    tn = 1024
    return pl.pallas_call(
        _ada_kernel,
        grid=(cols // tn,),
        in_specs=[pl.BlockSpec((bsz, d), lambda j: (0, 0)),
                  pl.BlockSpec((d, tn), lambda j: (0, j)),
                  pl.BlockSpec((1, tn), lambda j: (0, j))],
        out_specs=pl.BlockSpec((bsz, tn), lambda j: (0, j)),
        out_shape=jax.ShapeDtypeStruct((bsz, cols), F32),
        compiler_params=pltpu.CompilerParams(dimension_semantics=("arbitrary",)),
    )(c, w_ada, b_ada.reshape(1, cols))


def _in_kernel(tiles_per_seq,
               x_ref, pos_ref, mod_ref, n1w_ref,
               wqk_ref, wv_ref, wo_ref, wg_ref, wqd_ref, wkd_ref, wvd_ref, wgm_ref, wgd_ref,
               cw_ref, cb_ref, gb_ref, qnw_ref, knw_ref, freq_ref, bd_ref,
               qm_ref, km_ref, vm_ref, so_ref, gates_ref, qd_ref, kd_ref, vd_ref, sgm_ref, sgd_ref,
               cbuf):
    tm = x_ref.shape[0]
    i = pl.program_id(0)

    x = x_ref[...]
    ms = jnp.mean(x * x, axis=-1, keepdims=True)
    scale1 = mod_ref[0, 1:2, :]
    shift1 = mod_ref[0, 0:1, :]
    h = (x * lax.rsqrt(ms + EPS) * n1w_ref[...]) * (1.0 + scale1) + shift1
    hb = h.astype(BF16)

    y = _dot(hb, wqk_ref[...])

    @pl.when(i % tiles_per_seq == 0)
    def _():
        cbuf[0:SUBLANES, :] = jnp.zeros((SUBLANES, 2 * M_WIDTH), F32)

    cbuf[SUBLANES:SUBLANES + tm, :] = y
    acc = cb_ref[...] + cw_ref[3:4, :] * y
    for d in range(1, CONV_WIDTH):
        acc = acc + cw_ref[3 - d:4 - d, :] * cbuf[SUBLANES - d:SUBLANES - d + tm, :]
    cbuf[0:SUBLANES, :] = cbuf[tm:tm + SUBLANES, :]
    a = acc * _sigmoid(acc)
    qm_ref[...] = a[:, :M_WIDTH].astype(BF16)
    km_ref[...] = (a[:, M_WIDTH:] * (M_HEAD_DIM ** -0.5)).astype(BF16)

    vm_ref[...] = _dot(hb, wv_ref[...]).astype(BF16)
    so_ref[...] = _sigmoid(_dot(hb, wo_ref[...])).astype(BF16)

    g = _dot(hb, wg_ref[...]) + gb_ref[...]
    lane = lax.broadcasted_iota(jnp.int32, g.shape, 1)
    logsig = jnp.minimum(g, 0.0) - jnp.log(1.0 + jnp.exp(-jnp.abs(g)))
    gates_ref[...] = jnp.where(lane < M_HEADS, g, logsig)

    ang = pos_ref[...].astype(F32) * freq_ref[...]
    cos = jnp.cos(ang)
    sin = jnp.sin(ang)
    lane_in_map = lax.broadcasted_iota(jnp.int32, ang.shape, 1) % D_HEAD_DIM
    half = ROPE_DIM // 2
    s_from_lo = jnp.where(lane_in_map >= half, sin, 0.0)
    s_from_hi = jnp.where(lane_in_map < half, -sin, 0.0)
    reps = D_WIDTH // LANES
    cos_w = jnp.concatenate([cos] * reps, axis=1)
    slo_w = jnp.concatenate([s_from_lo] * reps, axis=1)
    shi_w = jnp.concatenate([s_from_hi] * reps, axis=1)

    def qk_norm_rope(w_ref, nw_ref, out_scale):
        t = _dot(hb, w_ref[...])
        sq = (t * t).astype(BF16)
        gs = [_dot(sq[:, c * MXU_DIM:(c + 1) * MXU_DIM], bd_ref[...]) for c in range(D_WIDTH // MXU_DIM)]
        msq = jnp.concatenate(gs, axis=1) * (1.0 / D_HEAD_DIM)
        tn = t * lax.rsqrt(msq + EPS) * nw_ref[...]
        rot = (tn * cos_w + pltpu.roll(tn, half, 1) * slo_w
               + pltpu.roll(tn, D_WIDTH - half, 1) * shi_w)
        return (rot * out_scale).astype(BF16)

    qd_ref[...] = qk_norm_rope(wqd_ref, qnw_ref, D_HEAD_DIM ** -0.5 * LOG2_E)
    kd_ref[...] = qk_norm_rope(wkd_ref, knw_ref, 1.0)
    vd_ref[...] = _dot(hb, wvd_ref[...]).astype(BF16)
    sgm_ref[...] = _sigmoid(_dot(hb, wgm_ref[...])).astype(BF16)
    sgd_ref[...] = _sigmoid(_dot(hb, wgd_ref[...])).astype(BF16)


def _in_call(x2, pos2, mod3, n1w, weights, smalls, seq):
    n, d = x2.shape
    tm = TM_IN
    tiles_per_seq = seq // tm
    row = lambda w: pl.BlockSpec((tm, w), lambda i: (i, 0))
    in_specs = [row(d), row(1),
                pl.BlockSpec((1, 6, d), lambda i: (i // tiles_per_seq, 0, 0)),
                _const_spec(n1w.shape)]
    in_specs += [_const_spec(w.shape) for w in weights]
    in_specs += [_const_spec(s.shape) for s in smalls]
    out_widths = [(M_WIDTH, BF16), (M_WIDTH, BF16), (M_WIDTH, BF16), (M_WIDTH, BF16), (LANES, F32),
                  (D_WIDTH, BF16), (D_WIDTH, BF16), (D_WIDTH, BF16), (d, BF16), (d, BF16)]
    return pl.pallas_call(
        functools.partial(_in_kernel, tiles_per_seq),
        grid=(n // tm,),
        in_specs=in_specs,
        out_specs=[row(w) for w, _ in out_widths],
        out_shape=[jax.ShapeDtypeStruct((n, w), dt) for w, dt in out_widths],
        scratch_shapes=[pltpu.VMEM((tm + 2 * SUBLANES, 2 * M_WIDTH), F32)],
        compiler_params=pltpu.CompilerParams(dimension_semantics=("arbitrary",),
                                             vmem_limit_bytes=VMEM_LIMIT),
    )(x2, pos2, mod3, n1w, *weights, *smalls)


def _mlstm_kernel(grp, q_ref, k_ref, v_ref, so_ref, gates_ref, nw_ref, out_ref, cx_ref, m_ref):
    @pl.when(pl.program_id(1) == 0)
    def _():
        cx_ref[...] = jnp.zeros(cx_ref.shape, F32)
        m_ref[...] = jnp.zeros(m_ref.shape, F32)

    for gi in range(grp):
        _mlstm_chunk(q_ref.at[gi], k_ref.at[gi], v_ref.at[gi], so_ref.at[gi], gates_ref.at[gi], nw_ref,
                     out_ref.at[gi], cx_ref.at[gi], m_ref.at[gi])


def _mlstm_chunk(q_ref, k_ref, v_ref, so_ref, gates_ref, nw_ref, out_ref, cx_ref, m_ref):
    L = q_ref.shape[0]
    hd = M_HEAD_DIM
    g = gates_ref[...]
    r_i = lax.broadcasted_iota(jnp.int32, (L, L), 0)
    c_i = lax.broadcasted_iota(jnp.int32, (L, L), 1)
    causal = c_i <= r_i
    bc = jnp.dot(causal.astype(F32), g, preferred_element_type=F32,
                 precision=lax.Precision.HIGHEST)
    g_t = g.T
    bc_t = bc.T
    ones_col = jnp.where(lax.broadcasted_iota(jnp.int32, (L, hd), 1) == 0, 1.0, 0.0).astype(BF16)

    for h in range(M_HEADS):
        sl = slice(h * hd, (h + 1) * hd)
        i_col = g[:, h:h + 1]
        b_col = bc[:, M_HEADS + h:M_HEADS + h + 1]
        i_row = g_t[h:h + 1, :]
        b_row = bc_t[M_HEADS + h:M_HEADS + h + 1, :]
        m_prev = m_ref[h][0:1, 0:1]
        cx = cx_ref[h]

        dmat = jnp.where(causal, b_col + (i_row - b_row), NEG_INF)
        m_inter = b_col + m_prev
        m_t = jnp.maximum(m_inter, jnp.max(dmat, axis=1, keepdims=True))
        q = q_ref[:, sl]
        k = k_ref[:, sl]
        vx = jnp.concatenate([v_ref[:, sl], ones_col], axis=1)
        s = _dot_nt(q, k) * jnp.exp(dmat - m_t)
        inter = jnp.exp(m_inter - m_t)
        numx = _dot(s.astype(BF16), vx) + inter * _dot(q, cx.astype(BF16))
        den = jnp.maximum(jnp.abs(numx[:, hd:hd + 1]), jnp.exp(-m_t))
        hh = numx[:, :hd] * (1.0 / den)

        b_last = b_col[L - 1:L, :]
        w_col = b_last + (i_col - b_col)
        m_new = jnp.maximum(b_last + m_prev, jnp.max(w_col, axis=0, keepdims=True))
        wexp = jnp.exp(w_col - m_new)
        decay = jnp.exp(b_last + m_prev - m_new)
        kw = (k.astype(F32) * wexp).astype(BF16)
        cx_ref[h] = decay * cx + _dot_tn(kw, vx)
        m_ref[h] = jnp.broadcast_to(m_new, m_ref.shape[1:])

        hn = hh * lax.rsqrt(jnp.mean(hh * hh, axis=-1, keepdims=True) + EPS) * nw_ref[:, sl]
        out_ref[:, sl] = (hn * so_ref[:, sl].astype(F32)).astype(BF16)


def _mlstm_call(qm, km, vm, so, gates, nw, bsz, seq):
    L = M_CHUNK
    grp = math.gcd(bsz, M_GROUP)
    blk = lambda w: pl.BlockSpec((grp, L, w), lambda b, c: (b, c, 0))
    seqs = lambda a: a.reshape(bsz, seq, a.shape[-1])
    out = pl.pallas_call(
        functools.partial(_mlstm_kernel, grp),
        grid=(bsz // grp, seq // L),
        in_specs=[blk(M_WIDTH), blk(M_WIDTH), blk(M_WIDTH), blk(M_WIDTH), blk(LANES),
                  pl.BlockSpec((1, M_WIDTH), lambda b, c: (0, 0))],
        out_specs=blk(M_WIDTH),
        out_shape=jax.ShapeDtypeStruct((bsz, seq, M_WIDTH), BF16),
        scratch_shapes=[pltpu.VMEM((grp, M_HEADS, M_HEAD_DIM, 2 * M_HEAD_DIM), F32),
                        pltpu.VMEM((grp, M_HEADS, SUBLANES, LANES), F32)],
        compiler_params=pltpu.CompilerParams(dimension_semantics=("arbitrary", "arbitrary"),
                                             vmem_limit_bytes=VMEM_LIMIT),
    )(seqs(qm), seqs(km), seqs(vm), seqs(so), seqs(gates), nw)
    return out.reshape(bsz * seq, M_WIDTH)


def _attn_kernel(ii_ref, jj_ref, q_ref, k_ref, v_ref, lam_ref, sw_ref, out_ref,
                 vt_ref, kp_ref, qa_ref, acc_ref, l_ref):
    seq = q_ref.shape[0]
    t = A_BLOCK
    nq = seq // t
    lp = lam_ref[...]
    lam = (jnp.exp(jnp.sum(lp[0:1] * lp[1:2], axis=-1, keepdims=True))
           - jnp.exp(jnp.sum(lp[2:3] * lp[3:4], axis=-1, keepdims=True)) + LAM_INIT)
    for jb in range(nq):
        vt_ref[jb] = v_ref[jb * t:(jb + 1) * t, :].T

    aux = (D_HEAD_DIM, 0)
    k = k_ref[...]
    ksq = jnp.square(k.astype(F32))
    lane_s = lax.broadcasted_iota(jnp.int32, (seq, LANES), 1)
    in_map_s = (lane_s < D_HEAD_DIM, lane_s >= D_HEAD_DIM)
    kmax2 = []
    for mp in range(2):
        kn2 = jnp.sum(jnp.where(in_map_s[mp], ksq, 0.0), axis=1, keepdims=True)
        kmax2.append(jnp.max(kn2, axis=0, keepdims=True))
        kp_ref[mp] = jnp.where(in_map_s[mp], k, jnp.where(lane_s == aux[mp], 1.0, 0.0).astype(BF16))

    q = q_ref[...].astype(F32)
    qsq = q * q
    for mp in range(2):
        qn2 = jnp.sum(jnp.where(in_map_s[mp], qsq, 0.0), axis=1, keepdims=True)
        bound = jnp.sqrt(qn2 * kmax2[mp]) * BOUND_SLACK
        qa_ref[mp] = jnp.where(in_map_s[mp], q, jnp.where(lane_s == aux[mp], -bound, 0.0)).astype(BF16)
    l_ref[...] = jnp.zeros(l_ref.shape, F32)
    acc_ref[...] = jnp.zeros(acc_ref.shape, F32)

    k_i = lax.broadcasted_iota(jnp.int32, (t, t), 0)
    q_i = lax.broadcasted_iota(jnp.int32, (t, t), 1)
    diag_mask = k_i <= q_i

    def step(i, j, masked):
        koff = pl.multiple_of(j * t, t)
        qoff = pl.multiple_of(i * t, t)
        vt = vt_ref[j]
        for mp in range(2):
            s = _dot_nt(kp_ref[mp, pl.ds(koff, t), :], qa_ref[mp, pl.ds(qoff, t), :])
            if masked:
                s = jnp.where(diag_mask, s, NEG_INF)
            p = jnp.exp2(s)
            l_ref[mp * nq + i] += jnp.sum(p, axis=0, keepdims=True)
            acc_ref[mp * nq + i] += _dot(vt, p.astype(BF16))

    def run(n_steps, step_at):
        trips = n_steps // A_UNROLL

        def body(u, c):
            for r in range(A_UNROLL):
                step_at(u * A_UNROLL + r)
            return c

        if trips:
            lax.fori_loop(0, trips, body, 0)
        for p in range(trips * A_UNROLL, n_steps):
            step_at(p)

    run(nq * (nq - 1) // 2, lambda p: step(ii_ref[p], jj_ref[p], False))
    run(nq, lambda p: step(p, p, True))

    def finish(i, c):
        o_t = (acc_ref[i] * (1.0 / l_ref[i]) - lam * (acc_ref[nq + i] * (1.0 / l_ref[nq + i])))
        nrm = lax.rsqrt(jnp.mean(o_t * o_t, axis=0, keepdims=True) + EPS)
        on = (o_t * nrm).T * sw_ref[...] * (1.0 - LAM_INIT)
        out_ref[pl.ds(pl.multiple_of(i * t, t), t), :] = on.astype(BF16)
        return c

    lax.fori_loop(0, nq, finish, 0)


def _attn_call(qd, kd, vd, lam_rows, subw, bsz, seq):
    t = A_BLOCK
    nq = seq // t
    pairs = [(i, j) for i in range(nq) for j in range(i)]
    ii = jnp.asarray([p[0] for p in pairs] or [0], jnp.int32)
    jj = jnp.asarray([p[1] for p in pairs] or [0], jnp.int32)
    blk = pl.BlockSpec((seq, LANES), lambda b, h, *_: (b, h))
    const = lambda shape: pl.BlockSpec(shape, lambda b, h, *_: (0,) * len(shape))
    return pl.pallas_call(
        _attn_kernel,
        grid_spec=pltpu.PrefetchScalarGridSpec(
            num_scalar_prefetch=2,
            grid=(bsz, D_HEADS),
            in_specs=[blk, blk, blk, const(lam_rows.shape), const(subw.shape)],
            out_specs=blk,
            scratch_shapes=[pltpu.VMEM((nq, D_V_DIM, t), BF16),
                            pltpu.VMEM((2, seq, LANES), BF16),
                            pltpu.VMEM((2, seq, LANES), BF16),
                            pltpu.VMEM((2 * nq, D_V_DIM, t), F32),
                            pltpu.VMEM((2 * nq, 1, t), F32)]),
        out_shape=jax.ShapeDtypeStruct((bsz * seq, D_WIDTH), BF16),
        compiler_params=pltpu.CompilerParams(dimension_semantics=("arbitrary", "arbitrary"),
                                             vmem_limit_bytes=VMEM_LIMIT),
    )(ii, jj, qd, kd, vd, lam_rows, subw)


def _merge_kernel(tiles_per_seq, x_ref, hm_ref, hd_ref, sgm_ref, sgd_ref, mod_ref, wbm_ref, wbd_ref, wout_ref,
                  n2w_ref, wr_hi_ref, wr_lo_ref, br_ref, x1_ref, h2_ref, route_ref, cnt_ref, carry_ref):
    gate1 = mod_ref[0, 2:3, :]
    shift2 = mod_ref[0, 3:4, :]
    scale2 = mod_ref[0, 4:5, :]
    merged = (sgm_ref[...].astype(F32) * _dot(hm_ref[...], wbm_ref[...])
              + sgd_ref[...].astype(F32) * _dot(hd_ref[...], wbd_ref[...]))
    x1 = x_ref[...] + gate1 * _dot(merged.astype(BF16), wout_ref[...])
    ms = jnp.mean(x1 * x1, axis=-1, keepdims=True)
    h2 = (x1 * lax.rsqrt(ms + EPS) * n2w_ref[...]) * (1.0 + scale2) + shift2
    tm = x1.shape[0]
    for ch in range(D_MODEL // LANES):
        cols = slice(ch * LANES, (ch + 1) * LANES)
        x1_ref[pl.ds(ch, tm, stride=SUBLANES), :] = x1[:, cols]
        h2_ref[pl.ds(ch, tm, stride=SUBLANES), :] = h2[:, cols]
    h2_hi = h2.astype(BF16)
    h2_lo = (h2 - h2_hi.astype(F32)).astype(BF16)
    logits = (_dot(h2_hi, wr_hi_ref[...]) + _dot(h2_lo, wr_hi_ref[...]) + _dot(h2_hi, wr_lo_ref[...])
              + br_ref[...])
    lane = lax.broadcasted_iota(jnp.int32, logits.shape, 1)
    lane_f = lane.astype(F32)
    big = float(LANES)

    def top(vals):
        mx = jnp.max(vals, axis=-1, keepdims=True)
        idx = jnp.min(jnp.where(vals == mx, lane_f, big), axis=-1, keepdims=True)
        return mx, idx

    is_group = lane < N_GROUPS
    gmax, gidx = top(jnp.where(is_group, logits, NEG_INF))
    gsum = jnp.sum(jnp.where(is_group, jnp.exp(logits - gmax), 0.0), axis=-1, keepdims=True)
    pg_top = 1.0 / gsum
    lo = N_GROUPS + EXPERTS_PER_GROUP * gidx
    in_group = (lane_f >= lo) & (lane_f < lo + EXPERTS_PER_GROUP)
    el = jnp.where(in_group, logits, NEG_INF)
    e1max, e1idx = top(el)
    e2max, e2idx = top(jnp.where(lane_f == e1idx, NEG_INF, el))
    p2 = jnp.exp(e2max - e1max)
    w1 = pg_top / (1.0 + p2)
    w2 = w1 * p2

    @pl.when(pl.program_id(0) % tiles_per_seq == 0)
    def _():
        carry_ref[...] = jnp.zeros(carry_ref.shape, F32)

    is_e1 = lane_f == e1idx
    is_e2 = lane_f == e2idx
    onehot = jnp.where(is_e1 | is_e2, 1.0, 0.0)
    r_i = lax.broadcasted_iota(jnp.int32, (tm, tm), 0)
    c_i = lax.broadcasted_iota(jnp.int32, (tm, tm), 1)
    before = jnp.where(c_i < r_i, 1.0, 0.0).astype(BF16)
    cum = _dot(before, onehot.astype(BF16)) + carry_ref[...]
    r1 = jnp.sum(jnp.where(is_e1, cum, 0.0), axis=-1, keepdims=True)
    r2 = jnp.sum(jnp.where(is_e2, cum, 0.0), axis=-1, keepdims=True)
    carry = carry_ref[...] + jnp.sum(onehot, axis=0, keepdims=True)
    carry_ref[...] = carry
    cnt_ref[0] = carry

    seq_len = float(tiles_per_seq * tm)
    pos = jnp.where(lane == 0, (e1idx - N_GROUPS) * seq_len + r1, (e2idx - N_GROUPS) * seq_len + r2)
    wbits = pltpu.bitcast(jnp.where(lane == 2, w1, jnp.where(lane == 3, w2, 0.0)), jnp.int32)
    route_ref[...] = jnp.where(lane < 2, pos.astype(jnp.int32), wbits)


def _merge_call(x2, hm, hd, sgm, sgd, mod3, wbm, wbd, wout, n2w, wr_hi, wr_lo, br, seq):
    n, d = x2.shape
    tm = TM_MERGE
    tiles_per_seq = seq // tm
    row = lambda w: pl.BlockSpec((tm, w), lambda i: (i, 0))
    consts = [wbm, wbd, wout, n2w, wr_hi, wr_lo, br]
    return pl.pallas_call(
        functools.partial(_merge_kernel, tiles_per_seq),
        grid=(n // tm,),
        in_specs=[row(d), row(M_WIDTH), row(D_WIDTH), row(d), row(d),
                  pl.BlockSpec((1, 6, d), lambda i: (i // tiles_per_seq, 0, 0))]
                 + [_const_spec(a.shape) for a in consts],
        out_specs=[pl.BlockSpec((tm * SUBLANES, LANES), lambda i: (i, 0)),
                   pl.BlockSpec((tm * SUBLANES, LANES), lambda i: (i, 0)),
                   row(LANES), pl.BlockSpec((1, 1, LANES), lambda i: (i, 0, 0))],
        out_shape=[jax.ShapeDtypeStruct((n * SUBLANES, LANES), F32), jax.ShapeDtypeStruct((n * SUBLANES, LANES), F32),
                   jax.ShapeDtypeStruct((n, LANES), jnp.int32),
                   jax.ShapeDtypeStruct((n // tm, 1, LANES), F32)],
        scratch_shapes=[pltpu.VMEM((1, LANES), F32)],
        compiler_params=pltpu.CompilerParams(dimension_semantics=("arbitrary",),
                                             vmem_limit_bytes=VMEM_LIMIT),
    )(x2, hm, hd, sgm, sgd, mod3, *consts)


def _moe_kernel(dest_hbm, rw_hbm, meta_hbm, h2_hbm, x1_hbm, mod_ref, wgu_hbm, wd_hbm, out_hbm,
                dest, rw, meta, inv, h2buf, acc, wgu_buf, wd_buf, xst, yst, ost, sems):
    b = pl.program_id(0)
    t_blk = h2buf.shape[0] // SUBLANES - 1
    spare = 2 * t_blk
    r = MOE_R
    tile = lambda ref, tok: ref.at[pl.ds(pl.multiple_of(tok * SUBLANES, SUBLANES), SUBLANES), :]

    def weight_copies(e, slot):
        return (pltpu.make_async_copy(wgu_hbm.at[e], wgu_buf.at[slot], sems.at[5 + slot]),
                pltpu.make_async_copy(wd_hbm.at[e], wd_buf.at[slot], sems.at[7 + slot]))

    route_copies = (pltpu.make_async_copy(dest_hbm.at[b], dest, sems.at[0]),
                    pltpu.make_async_copy(rw_hbm.at[b], rw, sems.at[1]),
                    pltpu.make_async_copy(meta_hbm.at[b], meta, sems.at[2]))
    h2_copy = pltpu.make_async_copy(h2_hbm.at[b], h2buf.at[pl.ds(0, t_blk * SUBLANES)], sems.at[3])
    x1_copy = pltpu.make_async_copy(x1_hbm.at[b], acc.at[pl.ds(0, t_blk * SUBLANES)], sems.at[4])

    for cp in route_copies + (h2_copy, x1_copy) + weight_copies(0, 0):
        cp.start()
    xst[...] = jnp.zeros(xst.shape, F32)
    yst[...] = jnp.zeros(yst.shape, F32)
    tile(h2buf, t_blk)[...] = jnp.zeros((SUBLANES, LANES), F32)
    tile(acc, t_blk)[...] = jnp.zeros((SUBLANES, LANES), F32)
    for cp in route_copies:
        cp.wait()

    last = inv.shape[0] - 1

    def pad_body(e, c):
        for j in range(MOE_U - 1):
            inv[jnp.minimum(e * t_blk + meta[e] + j, last)] = spare
        return c

    lax.fori_loop(0, N_EXPERTS, pad_body, 0)

    def inv_body(u, c):
        for j in range(MOE_U):
            s = u * MOE_U + j
            inv[dest[s]] = s
        return c

    lax.fori_loop(0, spare // MOE_U, inv_body, 0)
    h2_copy.wait()
    x1_copy.wait()
    gate2 = mod_ref[0, 5:6, :]
    n_chunks = D_MODEL // LANES

    def expert_body(e, c):
        slot = e % 2
        for cp in weight_copies(e, slot):
            cp.wait()

        @pl.when(e + 1 < N_EXPERTS)
        def _():
            for cp in weight_copies(e + 1, 1 - slot):
                cp.start()

        seg = e * t_blk
        n = meta[e]

        def tile_body(k, c2):
            base = seg + k * r
            groups = (jnp.minimum(n - k * r, r) + MOE_U - 1) // MOE_U

            def gather(u, c3):
                row0 = u * MOE_U
                pos0 = base + row0
                for j in range(MOE_U):
                    xst[pl.ds(row0 + j, SUBLANES, stride=MOE_RS), :] = tile(h2buf, inv[pos0 + j] >> 1)[...]
                return c3

            lax.fori_loop(0, groups, gather, 0)
            x = jnp.concatenate([xst[ch * MOE_RS:ch * MOE_RS + r, :] for ch in range(n_chunks)], axis=1)
            gu = _dot(x.astype(BF16), wgu_buf[slot])
            g = gu[:, :D_EXPERT]
            act = (g * _sigmoid(g) * gu[:, D_EXPERT:]).astype(BF16)
            y = _dot(act, wd_buf[slot]) * gate2
            for ch in range(n_chunks):
                yst[ch * MOE_RS:ch * MOE_RS + r, :] = y[:, ch * LANES:(ch + 1) * LANES]

            def scatter(u, c3):
                toks, vals = [], []
                row0 = u * MOE_U
                pos0 = base + row0
                for j in range(MOE_U):
                    v = inv[pos0 + j]
                    toks.append(v >> 1)
                    vals.append(tile(acc, v >> 1)[...]
                                + rw[v] * yst[pl.ds(row0 + j, SUBLANES, stride=MOE_RS), :])
                for tok, val in zip(toks, vals):
                    tile(acc, tok)[...] = val
                return c3

            lax.fori_loop(0, groups, scatter, 0)
            return c2

        lax.fori_loop(0, (n + r - 1) // r, tile_body, 0)
        return c

    lax.fori_loop(0, N_EXPERTS, expert_body, 0)

    rows = ost.shape[1]
    n_pieces = t_blk // rows

    def out_copy(p):
        return pltpu.make_async_copy(ost.at[p % 2], out_hbm.at[b, pl.ds(p * rows, rows)], sems.at[9 + p % 2])

    for p in range(n_pieces):
        if p >= 2:
            out_copy(p - 2).wait()
        for ch in range(n_chunks):
            ost[p % 2, :, ch * LANES:(ch + 1) * LANES] = acc[pl.ds(p * rows * SUBLANES + ch, rows, stride=SUBLANES), :]
        out_copy(p).start()
    for p in range(max(0, n_pieces - 2), n_pieces):
        out_copy(p).wait()


def _moe_call(dest, rw, meta, h2, x1, mod3, wgu, wd):
    nb = h2.shape[0]
    t_blk = h2.shape[1] // SUBLANES
    d = D_MODEL
    tok_rows = (t_blk + 1) * SUBLANES
    any_spec = pl.BlockSpec(memory_space=pl.ANY)
    return pl.pallas_call(
        _moe_kernel,
        grid=(nb,),
        in_specs=[any_spec, any_spec, any_spec, any_spec, any_spec,
                  pl.BlockSpec((1, 6, d), lambda i: (i, 0, 0)), any_spec, any_spec],
        out_specs=any_spec,
        out_shape=jax.ShapeDtypeStruct((nb, t_blk, d), F32),
        scratch_shapes=[pltpu.SMEM(dest.shape[1:], jnp.int32), pltpu.SMEM(rw.shape[1:], F32),
                        pltpu.SMEM(meta.shape[1:], jnp.int32), pltpu.SMEM((N_EXPERTS * t_blk,), jnp.int32),
                        pltpu.VMEM((tok_rows, LANES), F32), pltpu.VMEM((tok_rows, LANES), F32),
                        pltpu.VMEM((2, d, 2 * D_EXPERT), BF16), pltpu.VMEM((2, D_EXPERT, d), BF16),
                        pltpu.VMEM((SUBLANES * MOE_RS, LANES), F32), pltpu.VMEM((SUBLANES * MOE_RS, LANES), F32),
                        pltpu.VMEM((2, min(MOE_OUT_ROWS, t_blk), d), F32),
                        pltpu.SemaphoreType.DMA((11,))],
        compiler_params=pltpu.CompilerParams(dimension_semantics=("arbitrary",),
                                             vmem_limit_bytes=VMEM_LIMIT),
    )(dest, rw, meta, h2, x1, mod3, wgu, wd)


def _rope_freq_row():
    inv = ROPE_THETA ** (-np.arange(0, ROPE_DIM, 2, dtype=np.float32) / ROPE_DIM)
    row = np.zeros((LANES,), np.float32)
    half = ROPE_DIM // 2
    for base in range(0, LANES, D_HEAD_DIM):
        row[base:base + half] = inv
        row[base + half:base + ROPE_DIM] = inv
    return jnp.asarray(row.reshape(1, LANES))


def _block_diag_ones():
    idx = np.arange(MXU_DIM) // D_HEAD_DIM
    return jnp.asarray((idx[:, None] == idx[None, :]).astype(np.float32), dtype=BF16)


def kernel(x, c, positions, w_ada, b_ada, norm1_w, w_in, b_igate, b_fgate, conv_w, conv_b, mlstm_norm_w, q_norm_w, k_norm_w, lam_q1, lam_k1, lam_q2, lam_k2, subln_w, w_br_m, w_br_d, w_out, norm2_w, w_rg, b_rg, w_re, b_re, w_gate, w_up, w_down):
    bsz, seq, d = x.shape
    n = bsz * seq
    x2 = x.reshape(n, d)
    pos2 = positions.reshape(n, 1)
    l = 0

    mod3 = _ada_call(c, w_ada[l], b_ada[l]).reshape(bsz, 6, d)

    wi = w_in[l]
    o = 0
    segs = {}
    for name, width in (("qk", 2 * M_WIDTH), ("v", M_WIDTH), ("o", M_WIDTH), ("g", 2 * M_HEADS),
                        ("qd", D_WIDTH), ("kd", D_WIDTH), ("vd", D_WIDTH), ("gm", d), ("gd", d)):
        segs[name] = wi[:, o:o + width]
        o += width
    segs["g"] = jnp.pad(segs["g"], ((0, 0), (0, LANES - 2 * M_HEADS)))
    weights = [segs[k].astype(BF16) for k in ("qk", "v", "o", "g", "qd", "kd", "vd", "gm", "gd")]
    gate_bias = jnp.pad(jnp.concatenate([b_igate[l], b_fgate[l]]), (0, LANES - 2 * M_HEADS)).reshape(1, LANES)
    reps = D_WIDTH // D_HEAD_DIM
    smalls = [conv_w[l], conv_b[l].reshape(1, -1), gate_bias,
              jnp.tile(q_norm_w[l], reps).reshape(1, D_WIDTH), jnp.tile(k_norm_w[l], reps).reshape(1, D_WIDTH),
              _rope_freq_row(), _block_diag_ones()]
    qm, km, vm, so, gates, qd, kd, vd, sgm, sgd = _in_call(
        x2, pos2, mod3, norm1_w[l].reshape(1, d), weights, smalls, seq)

    hm = _mlstm_call(qm, km, vm, so, gates, mlstm_norm_w[l].reshape(1, M_WIDTH), bsz, seq)

    lam_rows = jnp.stack([lam_q1[l], lam_k1[l], lam_q2[l], lam_k2[l]])
    hd = _attn_call(qd, kd, vd, lam_rows, subln_w[l].reshape(1, D_V_DIM), bsz, seq)

    w_r = jnp.pad(jnp.concatenate([w_rg[l], w_re[l]], axis=1), ((0, 0), (0, LANES - N_GROUPS - N_EXPERTS)))
    wr_hi = w_r.astype(BF16)
    wr_lo = (w_r - wr_hi.astype(F32)).astype(BF16)
    b_r = jnp.pad(jnp.concatenate([b_rg[l], b_re[l]]), (0, LANES - N_GROUPS - N_EXPERTS)).reshape(1, LANES)
    x1, h2, route, cnt = _merge_call(x2, hm, hd, sgm, sgd, mod3, w_br_m[l].astype(BF16), w_br_d[l].astype(BF16),
                                     w_out[l].astype(BF16), norm2_w[l].reshape(1, d), wr_hi, wr_lo, b_r, seq)

    tiles_per_seq = seq // TM_MERGE
    meta = cnt[tiles_per_seq - 1::tiles_per_seq, 0, N_GROUPS:N_GROUPS + N_EXPERTS].astype(jnp.int32)
    meta = jnp.pad(meta, ((0, 0), (0, LANES - N_EXPERTS)))
    dest = route[:, 0:2].reshape(bsz, 2 * seq)
    rw = lax.bitcast_convert_type(route[:, 2:4], F32).reshape(bsz, 2 * seq)
    rw = jnp.pad(rw, ((0, 0), (0, LANES)))
    wgu = jnp.concatenate([w_gate[l], w_up[l]], axis=-1).astype(BF16)
    tok_tiles = (bsz, seq * SUBLANES, LANES)
    return _moe_call(dest, rw, meta, h2.reshape(tok_tiles), x1.reshape(tok_tiles), mod3, wgu,
                     w_down[l].astype(BF16))
```

```python
import functools
import math

import numpy as np
import jax
import jax.numpy as jnp
from jax import lax
from jax.experimental import pallas as pl
from jax.experimental.pallas import tpu as pltpu

F32 = jnp.float32
BF16 = jnp.bfloat16

D_MODEL = 1024
M_HEADS = 4
M_HEAD_DIM = 128
M_WIDTH = M_HEADS * M_HEAD_DIM
CONV_WIDTH = 4
D_HEADS = 4
D_HEAD_DIM = 64
D_V_DIM = 2 * D_HEAD_DIM
D_WIDTH = D_HEADS * D_V_DIM
ROPE_THETA = 500000.0
ROPE_DIM = D_HEAD_DIM // 4
N_GROUPS = 4
EXPERTS_PER_GROUP = 8
N_EXPERTS = N_GROUPS * EXPERTS_PER_GROUP
D_EXPERT = D_MODEL // 4
EPS = 1e-6
LAM_INIT = 0.8 - 0.6 * math.exp(-0.3 * 0)

LANES = 128
SUBLANES = 8
MXU_DIM = 256
VMEM_LIMIT = 56 * 1024 * 1024

TM_IN = 512
TM_MERGE = 512
M_CHUNK = 256
M_GROUP = 2
A_BLOCK = 512
LOG2_E = math.log2(math.e)
A_UNROLL = 4
BOUND_SLACK = 1.01
MOE_R = 256
MOE_RS = MOE_R + SUBLANES
MOE_U = 16
MOE_OUT_ROWS = 512
IN_SEGMENTS = (("qk", 2 * M_WIDTH), ("v", M_WIDTH), ("o", M_WIDTH), ("g", LANES),
               ("qd", D_WIDTH), ("kd", D_WIDTH), ("vd", D_WIDTH), ("gm", D_MODEL), ("gd", D_MODEL))
NEG_INF = float("-inf")


def _dot(a, b):
    return jnp.dot(a, b, preferred_element_type=F32)


def _dot_nt(a, b):
    return lax.dot_general(a, b, (((1,), (1,)), ((), ())), preferred_element_type=F32)


def _dot_tn(a, b):
    return lax.dot_general(a, b, (((0,), (0,)), ((), ())), preferred_element_type=F32)


def _sigmoid(v):
    return 0.5 * jnp.tanh(0.5 * v) + 0.5


def _const_spec(shape):
    nd = len(shape)
    return pl.BlockSpec(shape, lambda *_: (0,) * nd)


def _ada_kernel(c_ref, w_ref, b_ref, o_ref):
    o_ref[...] = _dot(c_ref[...].astype(BF16), w_ref[...].astype(BF16)) + b_ref[...]


def _ada_call(c, w_ada, b_ada):
    bsz, d = c.shape
    cols = w_ada.shape[1]
    tn = 1024
    return pl.pallas_call(
        _ada_kernel,
        grid=(cols // tn,),
        in_specs=[pl.BlockSpec((bsz, d), lambda j: (0, 0)),
                  pl.BlockSpec((d, tn), lambda j: (0, j)),
                  pl.BlockSpec((1, tn), lambda j: (0, j))],
        out_specs=pl.BlockSpec((bsz, tn), lambda j: (0, j)),
        out_shape=jax.ShapeDtypeStruct((bsz, cols), F32),
        compiler_params=pltpu.CompilerParams(dimension_semantics=("arbitrary",)),
    )(c, w_ada, b_ada.reshape(1, cols))


def _in_kernel(tiles_per_seq,
               x_ref, pos_ref, mod_ref, n1w_ref,
               w_ref, cw_ref, cb_ref, gb_ref, qnw_ref, knw_ref, freq_ref, bd_ref,
               qm_ref, km_ref, vm_ref, so_ref, gates_ref, qd_ref, kd_ref, vd_ref, sgm_ref, sgd_ref,
               cbuf):
    tm = x_ref.shape[0]
    i = pl.program_id(0)
    seg, o = {}, 0
    for name, width in IN_SEGMENTS:
        seg[name] = w_ref.at[:, o:o + width]
        o += width
    wqk_ref, wv_ref, wo_ref, wg_ref = seg["qk"], seg["v"], seg["o"], seg["g"]
    wqd_ref, wkd_ref, wvd_ref, wgm_ref, wgd_ref = seg["qd"], seg["kd"], seg["vd"], seg["gm"], seg["gd"]

    @pl.when(i == 0)
    def _():
        cbuf[...] = jnp.zeros(cbuf.shape, F32)

    x = x_ref[...]
    ms = jnp.mean(x * x, axis=-1, keepdims=True)
    scale1 = mod_ref[0, 1:2, :]
    shift1 = mod_ref[0, 0:1, :]
    h = (x * lax.rsqrt(ms + EPS) * n1w_ref[...]) * (1.0 + scale1) + shift1
    hb = h.astype(BF16)

    y = _dot(hb, wqk_ref[...])
    prev = jnp.where(i % tiles_per_seq == 0, 0.0, cbuf[...])
    cbuf[...] = y[tm - SUBLANES:, :]
    row8 = lax.broadcasted_iota(jnp.int32, (SUBLANES, 2 * M_WIDTH), 0)
    acc = cb_ref[...] + cw_ref[3:4, :] * y
    for d in range(1, CONV_WIDTH):
        shifted = pltpu.roll(y, d, 0)
        head = jnp.where(row8 < d, pltpu.roll(prev, d, 0), shifted[:SUBLANES])
        shifted = jnp.concatenate([head, shifted[SUBLANES:]], axis=0)
        acc = acc + cw_ref[3 - d:4 - d, :] * shifted
    a = acc * _sigmoid(acc)
    qm_ref[...] = a[:, :M_WIDTH].astype(BF16)
    km_ref[...] = (a[:, M_WIDTH:] * (M_HEAD_DIM ** -0.5)).astype(BF16)

    sgm_ref[...] = _sigmoid(_dot(hb, wgm_ref[...])).astype(BF16)
    vm_ref[...] = _dot(hb, wv_ref[...]).astype(BF16)
    so_ref[...] = _sigmoid(_dot(hb, wo_ref[...])).astype(BF16)

    g = _dot(hb, wg_ref[...]) + gb_ref[...]
    lane = lax.broadcasted_iota(jnp.int32, g.shape, 1)
    logsig = jnp.minimum(g, 0.0) - jnp.log(1.0 + jnp.exp(-jnp.abs(g)))
    gates_ref[...] = jnp.where(lane < M_HEADS, g, logsig)

    ang = pos_ref[...].astype(F32) * freq_ref[...]
    cos = jnp.cos(ang)
    sin = jnp.sin(ang)
    lane_in_map = lax.broadcasted_iota(jnp.int32, ang.shape, 1) % D_HEAD_DIM
    half = ROPE_DIM // 2
    s_from_lo = jnp.where(lane_in_map >= half, sin, 0.0)
    s_from_hi = jnp.where(lane_in_map < half, -sin, 0.0)
    reps = D_WIDTH // LANES
    cos_w = jnp.concatenate([cos] * reps, axis=1)
    slo_w = jnp.concatenate([s_from_lo] * reps, axis=1)
    shi_w = jnp.concatenate([s_from_hi] * reps, axis=1)

    def qk_norm_rope(w_ref, nw_ref, out_scale):
        t = _dot(hb, w_ref[...])
        sq = (t * t).astype(BF16)
        gs = [_dot(sq[:, c * MXU_DIM:(c + 1) * MXU_DIM], bd_ref[...]) for c in range(D_WIDTH // MXU_DIM)]
        msq = jnp.concatenate(gs, axis=1) * (1.0 / D_HEAD_DIM)
        tn = t * lax.rsqrt(msq + EPS) * nw_ref[...]
        rot = (tn * cos_w + pltpu.roll(tn, half, 1) * slo_w
               + pltpu.roll(tn, D_WIDTH - half, 1) * shi_w)
        return (rot * out_scale).astype(BF16)

    qd_ref[...] = qk_norm_rope(wqd_ref, qnw_ref, D_HEAD_DIM ** -0.5 * LOG2_E)
    sgd_ref[...] = _sigmoid(_dot(hb, wgd_ref[...])).astype(BF16)
    kd_ref[...] = qk_norm_rope(wkd_ref, knw_ref, 1.0)
    vd_ref[...] = _dot(hb, wvd_ref[...]).astype(BF16)


def _in_call(x2, pos2, mod3, n1w, w_packed, smalls, seq):
    n, d = x2.shape
    tm = TM_IN
    tiles_per_seq = seq // tm
    row = lambda w: pl.BlockSpec((tm, w), lambda i: (i, 0))
    in_specs = [row(d), row(1),
                pl.BlockSpec((1, 6, d), lambda i: (i // tiles_per_seq, 0, 0)),
                _const_spec(n1w.shape), _const_spec(w_packed.shape)]
    in_specs += [_const_spec(s.shape) for s in smalls]
    out_widths = [(M_WIDTH, BF16), (M_WIDTH, BF16), (M_WIDTH, BF16), (M_WIDTH, BF16), (LANES, F32),
                  (D_WIDTH, BF16), (D_WIDTH, BF16), (D_WIDTH, BF16), (d, BF16), (d, BF16)]
    return pl.pallas_call(
        functools.partial(_in_kernel, tiles_per_seq),
        grid=(n // tm,),
        in_specs=in_specs,
        out_specs=[row(w) for w, _ in out_widths],
        out_shape=[jax.ShapeDtypeStruct((n, w), dt) for w, dt in out_widths],
        scratch_shapes=[pltpu.VMEM((SUBLANES, 2 * M_WIDTH), F32)],
        compiler_params=pltpu.CompilerParams(dimension_semantics=("arbitrary",),
                                             vmem_limit_bytes=VMEM_LIMIT),
    )(x2, pos2, mod3, n1w, w_packed, *smalls)


def _mlstm_kernel(grp, q_ref, k_ref, v_ref, so_ref, gates_ref, nw_ref, out_ref, cx_ref, m_ref):
    @pl.when(pl.program_id(1) == 0)
    def _():
        cx_ref[...] = jnp.zeros(cx_ref.shape, F32)
        m_ref[...] = jnp.zeros(m_ref.shape, F32)

    for gi in range(grp):
        _mlstm_chunk(q_ref.at[gi], k_ref.at[gi], v_ref.at[gi], so_ref.at[gi], gates_ref.at[gi], nw_ref,
                     out_ref.at[gi], cx_ref.at[gi], m_ref.at[gi])


def _mlstm_chunk(q_ref, k_ref, v_ref, so_ref, gates_ref, nw_ref, out_ref, cx_ref, m_ref):
    L = q_ref.shape[0]
    hd = M_HEAD_DIM
    g = gates_ref[...]
    r_i = lax.broadcasted_iota(jnp.int32, (L, L), 0)
    c_i = lax.broadcasted_iota(jnp.int32, (L, L), 1)
    causal = c_i <= r_i
    bc = jnp.dot(causal.astype(F32), g, preferred_element_type=F32,
                 precision=lax.Precision.HIGHEST)
    g_t = g.T
    bc_t = bc.T
    ones_col = jnp.where(lax.broadcasted_iota(jnp.int32, (L, hd), 1) == 0, 1.0, 0.0).astype(BF16)

    for h in range(M_HEADS):
        sl = slice(h * hd, (h + 1) * hd)
        i_col = g[:, h:h + 1]
        b_col = bc[:, M_HEADS + h:M_HEADS + h + 1]
        i_row = g_t[h:h + 1, :]
        b_row = bc_t[M_HEADS + h:M_HEADS + h + 1, :]
        m_prev = m_ref[h][0:1, 0:1]
        cx = cx_ref[h]

        dmat = jnp.where(causal, b_col + (i_row - b_row), NEG_INF)
        m_inter = b_col + m_prev
        m_t = jnp.maximum(m_inter, jnp.max(dmat, axis=1, keepdims=True))
        q = q_ref[:, sl]
        k = k_ref[:, sl]
        vx = jnp.concatenate([v_ref[:, sl], ones_col], axis=1)
        s = _dot_nt(q, k) * jnp.exp(dmat - m_t)
        inter = jnp.exp(m_inter - m_t)
        numx = _dot(s.astype(BF16), vx) + inter * _dot(q, cx.astype(BF16))
        den = jnp.maximum(jnp.abs(numx[:, hd:hd + 1]), jnp.exp(-m_t))
        hh = numx[:, :hd] * (1.0 / den)

        b_last = b_col[L - 1:L, :]
        w_col = b_last + (i_col - b_col)
        m_new = jnp.maximum(b_last + m_prev, jnp.max(w_col, axis=0, keepdims=True))
        wexp = jnp.exp(w_col - m_new)
        decay = jnp.exp(b_last + m_prev - m_new)
        kw = (k.astype(F32) * wexp).astype(BF16)
        cx_ref[h] = decay * cx + _dot_tn(kw, vx)
        m_ref[h] = jnp.broadcast_to(m_new, m_ref.shape[1:])

        hn = hh * lax.rsqrt(jnp.mean(hh * hh, axis=-1, keepdims=True) + EPS) * nw_ref[:, sl]
        out_ref[:, sl] = (hn * so_ref[:, sl].astype(F32)).astype(BF16)


def _mlstm_call(qm, km, vm, so, gates, nw, bsz, seq):
    L = M_CHUNK
    grp = math.gcd(bsz, M_GROUP)
    blk = lambda w: pl.BlockSpec((grp, L, w), lambda b, c: (b, c, 0))
    seqs = lambda a: a.reshape(bsz, seq, a.shape[-1])
    out = pl.pallas_call(
        functools.partial(_mlstm_kernel, grp),
        grid=(bsz // grp, seq // L),
        in_specs=[blk(M_WIDTH), blk(M_WIDTH), blk(M_WIDTH), blk(M_WIDTH), blk(LANES),
                  pl.BlockSpec((1, M_WIDTH), lambda b, c: (0, 0))],
        out_specs=blk(M_WIDTH),
        out_shape=jax.ShapeDtypeStruct((bsz, seq, M_WIDTH), BF16),
        scratch_shapes=[pltpu.VMEM((grp, M_HEADS, M_HEAD_DIM, 2 * M_HEAD_DIM), F32),
                        pltpu.VMEM((grp, M_HEADS, SUBLANES, LANES), F32)],
        compiler_params=pltpu.CompilerParams(dimension_semantics=("arbitrary", "arbitrary"),
                                             vmem_limit_bytes=VMEM_LIMIT),
    )(seqs(qm), seqs(km), seqs(vm), seqs(so), seqs(gates), nw)
    return out.reshape(bsz * seq, M_WIDTH)


def _attn_kernel(ii_ref, jj_ref, q_ref, k_ref, v_ref, lam_ref, sw_ref, out_ref,
                 vt_ref, kp_ref, qa_ref, acc_ref, l_ref):
    seq = q_ref.shape[0]
    t = A_BLOCK
    nq = seq // t
    lp = lam_ref[...]
    lam = (jnp.exp(jnp.sum(lp[0:1] * lp[1:2], axis=-1, keepdims=True))
           - jnp.exp(jnp.sum(lp[2:3] * lp[3:4], axis=-1, keepdims=True)) + LAM_INIT)
    for jb in range(nq):
        vt_ref[jb] = v_ref[jb * t:(jb + 1) * t, :].T

    aux = (D_HEAD_DIM, 0)
    k = k_ref[...]
    ksq = jnp.square(k.astype(F32))
    lane_s = lax.broadcasted_iota(jnp.int32, (seq, LANES), 1)
    in_map_s = (lane_s < D_HEAD_DIM, lane_s >= D_HEAD_DIM)
    kmax2 = []
    for mp in range(2):
        kn2 = jnp.sum(jnp.where(in_map_s[mp], ksq, 0.0), axis=1, keepdims=True)
        kmax2.append(jnp.max(kn2, axis=0, keepdims=True))
        kp_ref[mp] = jnp.where(in_map_s[mp], k, jnp.where(lane_s == aux[mp], 1.0, 0.0).astype(BF16))

    q = q_ref[...].astype(F32)
    qsq = q * q
    for mp in range(2):
        qn2 = jnp.sum(jnp.where(in_map_s[mp], qsq, 0.0), axis=1, keepdims=True)
        bound = jnp.sqrt(qn2 * kmax2[mp]) * BOUND_SLACK
        qa_ref[mp] = jnp.where(in_map_s[mp], q, jnp.where(lane_s == aux[mp], -bound, 0.0)).astype(BF16)
    l_ref[...] = jnp.zeros(l_ref.shape, F32)
    acc_ref[...] = jnp.zeros(acc_ref.shape, F32)

    k_i = lax.broadcasted_iota(jnp.int32, (t, t), 0)
    q_i = lax.broadcasted_iota(jnp.int32, (t, t), 1)
    diag_mask = k_i <= q_i

    def step(i, j, masked):
        koff = pl.multiple_of(j * t, t)
        qoff = pl.multiple_of(i * t, t)
        vt = vt_ref[j]
        for mp in range(2):
            s = _dot_nt(kp_ref[mp, pl.ds(koff, t), :], qa_ref[mp, pl.ds(qoff, t), :])
            if masked:
                s = jnp.where(diag_mask, s, NEG_INF)
            p = jnp.exp2(s)
            l_ref[mp * nq + i] += jnp.sum(p, axis=0, keepdims=True)
            acc_ref[mp * nq + i] += _dot(vt, p.astype(BF16))

    def run(n_steps, step_at):
        trips = n_steps // A_UNROLL

        def body(u, c):
            for r in range(A_UNROLL):
                step_at(u * A_UNROLL + r)
            return c

        if trips:
            lax.fori_loop(0, trips, body, 0)
        for p in range(trips * A_UNROLL, n_steps):
            step_at(p)

    run(nq * (nq - 1) // 2, lambda p: step(ii_ref[p], jj_ref[p], False))
    run(nq, lambda p: step(p, p, True))

    def finish(i, c):
        o_t = (acc_ref[i] * (1.0 / l_ref[i]) - lam * (acc_ref[nq + i] * (1.0 / l_ref[nq + i])))
        nrm = lax.rsqrt(jnp.mean(o_t * o_t, axis=0, keepdims=True) + EPS)
        on = (o_t * nrm).T * sw_ref[...] * (1.0 - LAM_INIT)
        out_ref[pl.ds(pl.multiple_of(i * t, t), t), :] = on.astype(BF16)
        return c

    lax.fori_loop(0, nq, finish, 0)


def _attn_call(qd, kd, vd, lam_rows, subw, bsz, seq):
    t = A_BLOCK
    nq = seq // t
    pairs = [(i, j) for i in range(nq) for j in range(i)]
    ii = jnp.asarray([p[0] for p in pairs] or [0], jnp.int32)
    jj = jnp.asarray([p[1] for p in pairs] or [0], jnp.int32)
    blk = pl.BlockSpec((seq, LANES), lambda b, h, *_: (b, h))
    const = lambda shape: pl.BlockSpec(shape, lambda b, h, *_: (0,) * len(shape))
    return pl.pallas_call(
        _attn_kernel,
        grid_spec=pltpu.PrefetchScalarGridSpec(
            num_scalar_prefetch=2,
            grid=(bsz, D_HEADS),
            in_specs=[blk, blk, blk, const(lam_rows.shape), const(subw.shape)],
            out_specs=blk,
            scratch_shapes=[pltpu.VMEM((nq, D_V_DIM, t), BF16),
                            pltpu.VMEM((2, seq, LANES), BF16),
                            pltpu.VMEM((2, seq, LANES), BF16),
                            pltpu.VMEM((2 * nq, D_V_DIM, t), F32),
                            pltpu.VMEM((2 * nq, 1, t), F32)]),
        out_shape=jax.ShapeDtypeStruct((bsz * seq, D_WIDTH), BF16),
        compiler_params=pltpu.CompilerParams(dimension_semantics=("arbitrary", "arbitrary"),
                                             vmem_limit_bytes=VMEM_LIMIT),
    )(ii, jj, qd, kd, vd, lam_rows, subw)


def _merge_kernel(tiles_per_seq, x_ref, hm_ref, hd_ref, sgm_ref, sgd_ref, mod_ref, wbm_ref, wbd_ref, wout_ref,
                  n2w_ref, wr_hi_ref, wr_lo_ref, br_ref, x1_ref, h2_ref, dest_ref, rw_ref, cnt_ref, carry_ref):
    gate1 = mod_ref[0, 2:3, :]
    shift2 = mod_ref[0, 3:4, :]
    scale2 = mod_ref[0, 4:5, :]
    merged = (sgm_ref[...].astype(F32) * _dot(hm_ref[...], wbm_ref[...])
              + sgd_ref[...].astype(F32) * _dot(hd_ref[...], wbd_ref[...]))
    x1 = x_ref[...] + gate1 * _dot(merged.astype(BF16), wout_ref[...])
    ms = jnp.mean(x1 * x1, axis=-1, keepdims=True)
    h2 = (x1 * lax.rsqrt(ms + EPS) * n2w_ref[...]) * (1.0 + scale2) + shift2
    tm = x1.shape[0]
    for ch in range(D_MODEL // LANES):
        cols = slice(ch * LANES, (ch + 1) * LANES)
        x1_ref[pl.ds(ch, tm, stride=SUBLANES), :] = x1[:, cols]
        h2_ref[pl.ds(ch, tm, stride=SUBLANES), :] = h2[:, cols]
    h2_hi = h2.astype(BF16)
    h2_lo = (h2 - h2_hi.astype(F32)).astype(BF16)
    logits = (_dot(h2_hi, wr_hi_ref[...]) + _dot(h2_lo, wr_hi_ref[...]) + _dot(h2_hi, wr_lo_ref[...])
              + br_ref[...])
    lane = lax.broadcasted_iota(jnp.int32, logits.shape, 1)
    lane_f = lane.astype(F32)
    big = float(LANES)

    def top(vals):
        mx = jnp.max(vals, axis=-1, keepdims=True)
        idx = jnp.min(jnp.where(vals == mx, lane_f, big), axis=-1, keepdims=True)
        return mx, idx

    is_group = lane < N_GROUPS
    gmax, gidx = top(jnp.where(is_group, logits, NEG_INF))
    gsum = jnp.sum(jnp.where(is_group, jnp.exp(logits - gmax), 0.0), axis=-1, keepdims=True)
    pg_top = 1.0 / gsum
    lo = N_GROUPS + EXPERTS_PER_GROUP * gidx
    in_group = (lane_f >= lo) & (lane_f < lo + EXPERTS_PER_GROUP)
    el = jnp.where(in_group, logits, NEG_INF)
    e1max, e1idx = top(el)
    e2max, e2idx = top(jnp.where(lane_f == e1idx, NEG_INF, el))
    p2 = jnp.exp(e2max - e1max)
    w1 = pg_top / (1.0 + p2)
    w2 = w1 * p2

    @pl.when(pl.program_id(0) % tiles_per_seq == 0)
    def _():
        carry_ref[...] = jnp.zeros(carry_ref.shape, F32)

    is_e1 = lane_f == e1idx
    is_e2 = lane_f == e2idx
    onehot = jnp.where(is_e1 | is_e2, 1.0, 0.0)
    r_i = lax.broadcasted_iota(jnp.int32, (tm, tm), 0)
    c_i = lax.broadcasted_iota(jnp.int32, (tm, tm), 1)
    before = jnp.where(c_i < r_i, 1.0, 0.0).astype(BF16)
    cum = _dot(before, onehot.astype(BF16)) + carry_ref[...]
    r1 = jnp.sum(jnp.where(is_e1, cum, 0.0), axis=-1, keepdims=True)
    r2 = jnp.sum(jnp.where(is_e2, cum, 0.0), axis=-1, keepdims=True)
    carry = carry_ref[...] + jnp.sum(onehot, axis=0, keepdims=True)
    carry_ref[...] = carry
    cnt_ref[0] = carry

    seq_len = float(tiles_per_seq * tm)
    pos = jnp.where(lane == 0, (e1idx - N_GROUPS) * seq_len + r1, (e2idx - N_GROUPS) * seq_len + r2)
    dest_ref[...] = pos[:, 0:2].astype(jnp.int32)
    rw_ref[...] = jnp.where(lane == 0, w1, w2)[:, 0:2]


def _merge_call(x2, hm, hd, sgm, sgd, mod3, wbm, wbd, wout, n2w, wr_hi, wr_lo, br, seq):
    n, d = x2.shape
    tm = TM_MERGE
    tiles_per_seq = seq // tm
    row = lambda w: pl.BlockSpec((tm, w), lambda i: (i, 0))
    consts = [wbm, wbd, wout, n2w, wr_hi, wr_lo, br]
    return pl.pallas_call(
        functools.partial(_merge_kernel, tiles_per_seq),
        grid=(n // tm,),
        in_specs=[row(d), row(M_WIDTH), row(D_WIDTH), row(d), row(d),
                  pl.BlockSpec((1, 6, d), lambda i: (i // tiles_per_seq, 0, 0))]
                 + [_const_spec(a.shape) for a in consts],
        out_specs=[pl.BlockSpec((tm * SUBLANES, LANES), lambda i: (i, 0)),
                   pl.BlockSpec((tm * SUBLANES, LANES), lambda i: (i, 0)),
                   row(2), row(2), pl.BlockSpec((1, 1, LANES), lambda i: (i, 0, 0))],
        out_shape=[jax.ShapeDtypeStruct((n * SUBLANES, LANES), F32), jax.ShapeDtypeStruct((n * SUBLANES, LANES), F32),
                   jax.ShapeDtypeStruct((n, 2), jnp.int32), jax.ShapeDtypeStruct((n, 2), F32),
                   jax.ShapeDtypeStruct((n // tm, 1, LANES), F32)],
        scratch_shapes=[pltpu.VMEM((1, LANES), F32)],
        compiler_params=pltpu.CompilerParams(dimension_semantics=("arbitrary",),
                                             vmem_limit_bytes=VMEM_LIMIT),
    )(x2, hm, hd, sgm, sgd, mod3, *consts)


def _moe_kernel(dest_hbm, rw_hbm, meta_hbm, h2_hbm, x1_hbm, mod_ref, wgu_hbm, wd_hbm, out_hbm,
                dest, rw, meta, inv, h2buf, acc, wgu_buf, wd_buf, xst, yst, ost, sems):
    b = pl.program_id(0)
    t_blk = h2buf.shape[0] // SUBLANES - 1
    spare = 2 * t_blk
    r = MOE_R
    tile = lambda ref, tok: ref.at[pl.ds(pl.multiple_of(tok * SUBLANES, SUBLANES), SUBLANES), :]

    def weight_copies(e, slot):
        return (pltpu.make_async_copy(wgu_hbm.at[e], wgu_buf.at[slot], sems.at[5 + slot]),
                pltpu.make_async_copy(wd_hbm.at[e], wd_buf.at[slot], sems.at[7 + slot]))

    route_copies = (pltpu.make_async_copy(dest_hbm.at[b], dest, sems.at[0]),
                    pltpu.make_async_copy(rw_hbm.at[b], rw, sems.at[1]),
                    pltpu.make_async_copy(meta_hbm.at[b], meta, sems.at[2]))
    h2_copy = pltpu.make_async_copy(h2_hbm.at[b], h2buf.at[pl.ds(0, t_blk * SUBLANES)], sems.at[3])
    x1_copy = pltpu.make_async_copy(x1_hbm.at[b], acc.at[pl.ds(0, t_blk * SUBLANES)], sems.at[4])

    for cp in route_copies + (h2_copy, x1_copy) + weight_copies(0, 0):
        cp.start()
    xst[...] = jnp.zeros(xst.shape, F32)
    yst[...] = jnp.zeros(yst.shape, F32)
    tile(h2buf, t_blk)[...] = jnp.zeros((SUBLANES, LANES), F32)
    tile(acc, t_blk)[...] = jnp.zeros((SUBLANES, LANES), F32)
    for cp in route_copies:
        cp.wait()

    last = inv.shape[0] - 1

    def pad_body(e, c):
        for j in range(MOE_U - 1):
            inv[jnp.minimum(e * t_blk + meta[e] + j, last)] = spare
        return c

    lax.fori_loop(0, N_EXPERTS, pad_body, 0)

    def inv_body(u, c):
        for j in range(MOE_U):
            s = u * MOE_U + j
            inv[dest[s]] = s
        return c

    lax.fori_loop(0, spare // MOE_U, inv_body, 0)
    h2_copy.wait()
    x1_copy.wait()
    gate2 = mod_ref[0, 5:6, :]
    n_chunks = D_MODEL // LANES

    def expert_body(e, c):
        slot = e % 2
        for cp in weight_copies(e, slot):
            cp.wait()

        @pl.when(e + 1 < N_EXPERTS)
        def _():
            for cp in weight_copies(e + 1, 1 - slot):
                cp.start()

        seg = e * t_blk
        n = meta[e]

        def tile_body(k, c2):
            base = seg + k * r
            groups = (jnp.minimum(n - k * r, r) + MOE_U - 1) // MOE_U

            def gather(u, c3):
                row0 = u * MOE_U
                pos0 = base + row0
                for j in range(MOE_U):
                    xst[pl.ds(row0 + j, SUBLANES, stride=MOE_RS), :] = tile(h2buf, inv[pos0 + j] >> 1)[...]
                return c3

            lax.fori_loop(0, groups, gather, 0)
            x = jnp.concatenate([xst[ch * MOE_RS:ch * MOE_RS + r, :] for ch in range(n_chunks)], axis=1)
            gu = _dot(x.astype(BF16), wgu_buf[slot])
            g = gu[:, :D_EXPERT]
            act = (g * _sigmoid(g) * gu[:, D_EXPERT:]).astype(BF16)
            y = _dot(act, wd_buf[slot]) * gate2
            for ch in range(n_chunks):
                yst[ch * MOE_RS:ch * MOE_RS + r, :] = y[:, ch * LANES:(ch + 1) * LANES]

            def scatter(u, c3):
                toks, vals = [], []
                row0 = u * MOE_U
                pos0 = base + row0
                for j in range(MOE_U):
                    v = inv[pos0 + j]
                    toks.append(v >> 1)
                    vals.append(tile(acc, v >> 1)[...]
                                + rw[v] * yst[pl.ds(row0 + j, SUBLANES, stride=MOE_RS), :])
                for tok, val in zip(toks, vals):
                    tile(acc, tok)[...] = val
                return c3

            lax.fori_loop(0, groups, scatter, 0)
            return c2

        lax.fori_loop(0, (n + r - 1) // r, tile_body, 0)
        return c

    lax.fori_loop(0, N_EXPERTS, expert_body, 0)

    rows = ost.shape[1]
    n_pieces = t_blk // rows

    def out_copy(p):
        return pltpu.make_async_copy(ost.at[p % 2], out_hbm.at[b, pl.ds(p * rows, rows)], sems.at[9 + p % 2])

    for p in range(n_pieces):
        if p >= 2:
            out_copy(p - 2).wait()
        for ch in range(n_chunks):
            ost[p % 2, :, ch * LANES:(ch + 1) * LANES] = acc[pl.ds(p * rows * SUBLANES + ch, rows, stride=SUBLANES), :]
        out_copy(p).start()
    for p in range(max(0, n_pieces - 2), n_pieces):
        out_copy(p).wait()


def _moe_call(dest, rw, meta, h2, x1, mod3, wgu, wd):
    nb = h2.shape[0]
    t_blk = h2.shape[1] // SUBLANES
    d = D_MODEL
    tok_rows = (t_blk + 1) * SUBLANES
    any_spec = pl.BlockSpec(memory_space=pl.ANY)
    return pl.pallas_call(
        _moe_kernel,
        grid=(nb,),
        in_specs=[any_spec, any_spec, any_spec, any_spec, any_spec,
                  pl.BlockSpec((1, 6, d), lambda i: (i, 0, 0)), any_spec, any_spec],
        out_specs=any_spec,
        out_shape=jax.ShapeDtypeStruct((nb, t_blk, d), F32),
        scratch_shapes=[pltpu.SMEM(dest.shape[1:], jnp.int32), pltpu.SMEM(rw.shape[1:], F32),
                        pltpu.SMEM(meta.shape[1:], jnp.int32), pltpu.SMEM((N_EXPERTS * t_blk,), jnp.int32),
                        pltpu.VMEM((tok_rows, LANES), F32), pltpu.VMEM((tok_rows, LANES), F32),
                        pltpu.VMEM((2, d, 2 * D_EXPERT), BF16), pltpu.VMEM((2, D_EXPERT, d), BF16),
                        pltpu.VMEM((SUBLANES * MOE_RS, LANES), F32), pltpu.VMEM((SUBLANES * MOE_RS, LANES), F32),
                        pltpu.VMEM((2, min(MOE_OUT_ROWS, t_blk), d), F32),
                        pltpu.SemaphoreType.DMA((11,))],
        compiler_params=pltpu.CompilerParams(dimension_semantics=("arbitrary",),
                                             vmem_limit_bytes=VMEM_LIMIT),
    )(dest, rw, meta, h2, x1, mod3, wgu, wd)


def _rope_freq_row():
    inv = ROPE_THETA ** (-np.arange(0, ROPE_DIM, 2, dtype=np.float32) / ROPE_DIM)
    row = np.zeros((LANES,), np.float32)
    half = ROPE_DIM // 2
    for base in range(0, LANES, D_HEAD_DIM):
        row[base:base + half] = inv
        row[base + half:base + ROPE_DIM] = inv
    return jnp.asarray(row.reshape(1, LANES))


def _block_diag_ones():
    idx = np.arange(MXU_DIM) // D_HEAD_DIM
    return jnp.asarray((idx[:, None] == idx[None, :]).astype(np.float32), dtype=BF16)


def kernel(x, c, positions, w_ada, b_ada, norm1_w, w_in, b_igate, b_fgate, conv_w, conv_b, mlstm_norm_w, q_norm_w, k_norm_w, lam_q1, lam_k1, lam_q2, lam_k2, subln_w, w_br_m, w_br_d, w_out, norm2_w, w_rg, b_rg, w_re, b_re, w_gate, w_up, w_down):
    bsz, seq, d = x.shape
    n = bsz * seq
    x2 = x.reshape(n, d)
    pos2 = positions.reshape(n, 1)
    l = 0

    mod3 = _ada_call(c, w_ada[l], b_ada[l]).reshape(bsz, 6, d)

    wi = w_in[l]
    gate_lo = 4 * M_WIDTH
    gate_hi = gate_lo + 2 * M_HEADS
    w_packed = jnp.concatenate(
        [wi[:, :gate_lo].astype(BF16),
         jnp.pad(wi[:, gate_lo:gate_hi], ((0, 0), (0, LANES - 2 * M_HEADS))).astype(BF16),
         wi[:, gate_hi:].astype(BF16)], axis=1)
    gate_bias = jnp.pad(jnp.concatenate([b_igate[l], b_fgate[l]]), (0, LANES - 2 * M_HEADS)).reshape(1, LANES)
    reps = D_WIDTH // D_HEAD_DIM
    smalls = [conv_w[l], conv_b[l].reshape(1, -1), gate_bias,
              jnp.tile(q_norm_w[l], reps).reshape(1, D_WIDTH), jnp.tile(k_norm_w[l], reps).reshape(1, D_WIDTH),
              _rope_freq_row(), _block_diag_ones()]
    qm, km, vm, so, gates, qd, kd, vd, sgm, sgd = _in_call(
        x2, pos2, mod3, norm1_w[l].reshape(1, d), w_packed, smalls, seq)

    hm = _mlstm_call(qm, km, vm, so, gates, mlstm_norm_w[l].reshape(1, M_WIDTH), bsz, seq)

    lam_rows = jnp.stack([lam_q1[l], lam_k1[l], lam_q2[l], lam_k2[l]])
    hd = _attn_call(qd, kd, vd, lam_rows, subln_w[l].reshape(1, D_V_DIM), bsz, seq)

    w_r = jnp.pad(jnp.concatenate([w_rg[l], w_re[l]], axis=1), ((0, 0), (0, LANES - N_GROUPS - N_EXPERTS)))
    wr_hi = w_r.astype(BF16)
    wr_lo = (w_r - wr_hi.astype(F32)).astype(BF16)
    b_r = jnp.pad(jnp.concatenate([b_rg[l], b_re[l]]), (0, LANES - N_GROUPS - N_EXPERTS)).reshape(1, LANES)
    x1, h2, dest, rw, cnt = _merge_call(x2, hm, hd, sgm, sgd, mod3, w_br_m[l].astype(BF16), w_br_d[l].astype(BF16),
                                        w_out[l].astype(BF16), norm2_w[l].reshape(1, d), wr_hi, wr_lo, b_r, seq)

    tiles_per_seq = seq // TM_MERGE
    meta = cnt[tiles_per_seq - 1::tiles_per_seq, 0, N_GROUPS:N_GROUPS + N_EXPERTS].astype(jnp.int32)
    meta = jnp.pad(meta, ((0, 0), (0, LANES - N_EXPERTS)))
    dest = dest.reshape(bsz, 2 * seq)
    rw = jnp.pad(rw.reshape(bsz, 2 * seq), ((0, 0), (0, LANES)))
    wgu = jnp.concatenate([w_gate[l], w_up[l]], axis=-1).astype(BF16)
    tok_tiles = (bsz, seq * SUBLANES, LANES)
    return _moe_call(dest, rw, meta, h2.reshape(tok_tiles), x1.reshape(tok_tiles), mod3, wgu,
                     w_down[l].astype(BF16))
```

```python
import functools
import math

import numpy as np
import jax
import jax.numpy as jnp
from jax import lax
from jax.experimental import pallas as pl
from jax.experimental.pallas import tpu as pltpu

F32 = jnp.float32
BF16 = jnp.bfloat16

D_MODEL = 1024
M_HEADS = 4
M_HEAD_DIM = 128
M_WIDTH = M_HEADS * M_HEAD_DIM
CONV_WIDTH = 4
D_HEADS = 4
D_HEAD_DIM = 64
D_V_DIM = 2 * D_HEAD_DIM
D_WIDTH = D_HEADS * D_V_DIM
ROPE_THETA = 500000.0
ROPE_DIM = D_HEAD_DIM // 4
N_GROUPS = 4
EXPERTS_PER_GROUP = 8
N_EXPERTS = N_GROUPS * EXPERTS_PER_GROUP
D_EXPERT = D_MODEL // 4
EPS = 1e-6
LAM_INIT = 0.8 - 0.6 * math.exp(-0.3 * 0)

LANES = 128
SUBLANES = 8
MXU_DIM = 256
VMEM_LIMIT = 56 * 1024 * 1024

TM_IN = 512
TM_MERGE = 512
M_CHUNK = 256
M_GROUP = 4
A_BLOCK = 512
LOG2_E = math.log2(math.e)
A_UNROLL = 4
BOUND_SLACK = 1.01
MOE_R = 256
MOE_RS = MOE_R + SUBLANES
MOE_U = 16
MOE_OUT_ROWS = 512
IN_SEGMENTS = (("qk", 2 * M_WIDTH), ("v", M_WIDTH), ("o", M_WIDTH), ("g", LANES),
               ("qd", D_WIDTH), ("kd", D_WIDTH), ("vd", D_WIDTH), ("gm", D_MODEL), ("gd", D_MODEL))
NEG_INF = float("-inf")


def _dot(a, b):
    return jnp.dot(a, b, preferred_element_type=F32)


def _dot_nt(a, b):
    return lax.dot_general(a, b, (((1,), (1,)), ((), ())), preferred_element_type=F32)


def _dot_tn(a, b):
    return lax.dot_general(a, b, (((0,), (0,)), ((), ())), preferred_element_type=F32)


def _sigmoid(v):
    return 0.5 * jnp.tanh(0.5 * v) + 0.5


def _const_spec(shape):
    nd = len(shape)
    return pl.BlockSpec(shape, lambda *_: (0,) * nd)


def _ada_kernel(c_ref, w_ref, b_ref, o_ref):
    o_ref[...] = _dot(c_ref[...].astype(BF16), w_ref[...].astype(BF16)) + b_ref[...]


def _ada_call(c, w_ada, b_ada):
    bsz, d = c.shape
    cols = w_ada.shape[1]
    tn = 1024
    return pl.pallas_call(
        _ada_kernel,
        grid=(cols // tn,),
        in_specs=[pl.BlockSpec((bsz, d), lambda j: (0, 0)),
                  pl.BlockSpec((d, tn), lambda j: (0, j)),
                  pl.BlockSpec((1, tn), lambda j: (0, j))],
        out_specs=pl.BlockSpec((bsz, tn), lambda j: (0, j)),
        out_shape=jax.ShapeDtypeStruct((bsz, cols), F32),
        compiler_params=pltpu.CompilerParams(dimension_semantics=("arbitrary",)),
    )(c, w_ada, b_ada.reshape(1, cols))


def _in_kernel(tiles_per_seq,
               x_ref, pos_ref, mod_ref, n1w_ref,
               w_ref, cw_ref, cb_ref, gb_ref, qnw_ref, knw_ref, freq_ref, bd_ref,
               qm_ref, km_ref, vm_ref, so_ref, gates_ref, qd_ref, kd_ref, vd_ref, sgm_ref, sgd_ref,
               cbuf):
    tm = x_ref.shape[0]
    i = pl.program_id(0)
    seg, o = {}, 0
    for name, width in IN_SEGMENTS:
        seg[name] = w_ref.at[:, o:o + width]
        o += width
    wqk_ref, wv_ref, wo_ref, wg_ref = seg["qk"], seg["v"], seg["o"], seg["g"]
    wqd_ref, wkd_ref, wvd_ref, wgm_ref, wgd_ref = seg["qd"], seg["kd"], seg["vd"], seg["gm"], seg["gd"]

    @pl.when(i == 0)
    def _():
        cbuf[...] = jnp.zeros(cbuf.shape, F32)

    x = x_ref[...]
    ms = jnp.mean(x * x, axis=-1, keepdims=True)
    scale1 = mod_ref[0, 1:2, :]
    shift1 = mod_ref[0, 0:1, :]
    h = (x * lax.rsqrt(ms + EPS) * n1w_ref[...]) * (1.0 + scale1) + shift1
    hb = h.astype(BF16)

    y = _dot(hb, wqk_ref[...])
    prev = jnp.where(i % tiles_per_seq == 0, 0.0, cbuf[...])
    cbuf[...] = y[tm - SUBLANES:, :]
    row8 = lax.broadcasted_iota(jnp.int32, (SUBLANES, 2 * M_WIDTH), 0)
    acc = cb_ref[...] + cw_ref[3:4, :] * y
    for d in range(1, CONV_WIDTH):
        shifted = pltpu.roll(y, d, 0)
        head = jnp.where(row8 < d, pltpu.roll(prev, d, 0), shifted[:SUBLANES])
        shifted = jnp.concatenate([head, shifted[SUBLANES:]], axis=0)
        acc = acc + cw_ref[3 - d:4 - d, :] * shifted
    a = acc * _sigmoid(acc)
    qm_ref[...] = a[:, :M_WIDTH].astype(BF16)
    km_ref[...] = (a[:, M_WIDTH:] * (M_HEAD_DIM ** -0.5)).astype(BF16)

    sgm_ref[...] = _sigmoid(_dot(hb, wgm_ref[...])).astype(BF16)
    vm_ref[...] = _dot(hb, wv_ref[...]).astype(BF16)
    so_ref[...] = _sigmoid(_dot(hb, wo_ref[...])).astype(BF16)

    g = _dot(hb, wg_ref[...]) + gb_ref[...]
    lane = lax.broadcasted_iota(jnp.int32, g.shape, 1)
    logsig = jnp.minimum(g, 0.0) - jnp.log(1.0 + jnp.exp(-jnp.abs(g)))
    gates_ref[...] = jnp.where(lane < M_HEADS, g, logsig)

    ang = pos_ref[...].astype(F32) * freq_ref[...]
    cos = jnp.cos(ang)
    sin = jnp.sin(ang)
    lane_in_map = lax.broadcasted_iota(jnp.int32, ang.shape, 1) % D_HEAD_DIM
    half = ROPE_DIM // 2
    s_from_lo = jnp.where(lane_in_map >= half, sin, 0.0)
    s_from_hi = jnp.where(lane_in_map < half, -sin, 0.0)
    reps = D_WIDTH // LANES
    cos_w = jnp.concatenate([cos] * reps, axis=1)
    slo_w = jnp.concatenate([s_from_lo] * reps, axis=1)
    shi_w = jnp.concatenate([s_from_hi] * reps, axis=1)

    def qk_norm_rope(w_ref, nw_ref, out_scale):
        t = _dot(hb, w_ref[...])
        sq = (t * t).astype(BF16)
        gs = [_dot(sq[:, c * MXU_DIM:(c + 1) * MXU_DIM], bd_ref[...]) for c in range(D_WIDTH // MXU_DIM)]
        msq = jnp.concatenate(gs, axis=1) * (1.0 / D_HEAD_DIM)
        tn = t * lax.rsqrt(msq + EPS) * nw_ref[...]
        rot = (tn * cos_w + pltpu.roll(tn, half, 1) * slo_w
               + pltpu.roll(tn, D_WIDTH - half, 1) * shi_w)
        return (rot * out_scale).astype(BF16)

    qd_ref[...] = qk_norm_rope(wqd_ref, qnw_ref, D_HEAD_DIM ** -0.5 * LOG2_E)
    sgd_ref[...] = _sigmoid(_dot(hb, wgd_ref[...])).astype(BF16)
    kd_ref[...] = qk_norm_rope(wkd_ref, knw_ref, 1.0)
    vd_ref[...] = _dot(hb, wvd_ref[...]).astype(BF16)


def _in_call(x2, pos2, mod3, n1w, w_packed, smalls, seq):
    n, d = x2.shape
    tm = TM_IN
    tiles_per_seq = seq // tm
    row = lambda w: pl.BlockSpec((tm, w), lambda i: (i, 0))
    in_specs = [row(d), row(1),
                pl.BlockSpec((1, 6, d), lambda i: (i // tiles_per_seq, 0, 0)),
                _const_spec(n1w.shape), _const_spec(w_packed.shape)]
    in_specs += [_const_spec(s.shape) for s in smalls]
    out_widths = [(M_WIDTH, BF16), (M_WIDTH, BF16), (M_WIDTH, BF16), (M_WIDTH, BF16), (LANES, F32),
                  (D_WIDTH, BF16), (D_WIDTH, BF16), (D_WIDTH, BF16), (d, BF16), (d, BF16)]
    return pl.pallas_call(
        functools.partial(_in_kernel, tiles_per_seq),
        grid=(n // tm,),
        in_specs=in_specs,
        out_specs=[row(w) for w, _ in out_widths],
        out_shape=[jax.ShapeDtypeStruct((n, w), dt) for w, dt in out_widths],
        scratch_shapes=[pltpu.VMEM((SUBLANES, 2 * M_WIDTH), F32)],
        compiler_params=pltpu.CompilerParams(dimension_semantics=("arbitrary",),
                                             vmem_limit_bytes=VMEM_LIMIT),
    )(x2, pos2, mod3, n1w, w_packed, *smalls)


def _mlstm_kernel(grp, q_ref, k_ref, v_ref, so_ref, gates_ref, nw_ref, out_ref, cx_ref, m_ref):
    @pl.when(pl.program_id(1) == 0)
    def _():
        cx_ref[...] = jnp.zeros(cx_ref.shape, F32)
        m_ref[...] = jnp.zeros(m_ref.shape, F32)

    for gi in range(grp):
        _mlstm_chunk(q_ref.at[gi], k_ref.at[gi], v_ref.at[gi], so_ref.at[gi], gates_ref.at[gi], nw_ref,
                     out_ref.at[gi], cx_ref.at[gi], m_ref.at[gi])


def _mlstm_chunk(q_ref, k_ref, v_ref, so_ref, gates_ref, nw_ref, out_ref, cx_ref, m_ref):
    L = q_ref.shape[0]
    hd = M_HEAD_DIM
    g = gates_ref[...]
    r_i = lax.broadcasted_iota(jnp.int32, (L, L), 0)
    c_i = lax.broadcasted_iota(jnp.int32, (L, L), 1)
    causal = c_i <= r_i
    bc = jnp.dot(causal.astype(F32), g, preferred_element_type=F32,
                 precision=lax.Precision.HIGHEST)
    g_t = g.T
    bc_t = bc.T
    ones_col = jnp.where(lax.broadcasted_iota(jnp.int32, (L, hd), 1) == 0, 1.0, 0.0).astype(BF16)

    causal_t = r_i <= c_i
    for h in range(M_HEADS):
        sl = slice(h * hd, (h + 1) * hd)
        i_row = g_t[h:h + 1, :]
        b_row = bc_t[M_HEADS + h:M_HEADS + h + 1, :]
        r_col = g[:, h:h + 1] - bc[:, M_HEADS + h:M_HEADS + h + 1]
        m_prev = m_ref[h][0:1, 0:1]
        cxt = cx_ref[h]

        d_t = jnp.where(causal_t, b_row + r_col, NEG_INF)
        m_inter = b_row + m_prev
        m_row = jnp.maximum(m_inter, jnp.max(d_t, axis=0, keepdims=True))
        q = q_ref[:, sl]
        k = k_ref[:, sl]
        vx = jnp.concatenate([v_ref[:, sl], ones_col], axis=1)
        s_t = _dot_nt(k, q) * jnp.exp(d_t - m_row)
        inter = jnp.exp(m_inter - m_row)
        num_t = _dot_tn(vx, s_t.astype(BF16)) + inter * _dot_nt(cxt.astype(BF16), q)
        den = jnp.maximum(jnp.abs(num_t[hd:hd + 1, :]), jnp.exp(-m_row))
        h_t = num_t[:hd, :] * (1.0 / den)

        b_last = b_row[:, L - 1:L]
        m_new = jnp.maximum(b_last + m_prev, jnp.max(b_last + (i_row - b_row), axis=1, keepdims=True))
        wexp = jnp.exp(b_last + r_col - m_new)
        decay = jnp.exp(b_last + m_prev - m_new)
        kw = (k.astype(F32) * wexp).astype(BF16)
        cx_ref[h] = decay * cxt + _dot_tn(vx, kw)
        m_ref[h] = jnp.broadcast_to(m_new, m_ref.shape[1:])

        hn_t = h_t * lax.rsqrt(jnp.mean(h_t * h_t, axis=0, keepdims=True) + EPS)
        out_ref[:, sl] = (hn_t.T * nw_ref[:, sl] * so_ref[:, sl].astype(F32)).astype(BF16)


def _mlstm_call(qm, km, vm, so, gates, nw, bsz, seq):
    L = M_CHUNK
    grp = math.gcd(bsz, M_GROUP)
    blk = lambda w: pl.BlockSpec((grp, L, w), lambda b, c: (b, c, 0))
    seqs = lambda a: a.reshape(bsz, seq, a.shape[-1])
    out = pl.pallas_call(
        functools.partial(_mlstm_kernel, grp),
        grid=(bsz // grp, seq // L),
        in_specs=[blk(M_WIDTH), blk(M_WIDTH), blk(M_WIDTH), blk(M_WIDTH), blk(LANES),
                  pl.BlockSpec((1, M_WIDTH), lambda b, c: (0, 0))],
        out_specs=blk(M_WIDTH),
        out_shape=jax.ShapeDtypeStruct((bsz, seq, M_WIDTH), BF16),
        scratch_shapes=[pltpu.VMEM((grp, M_HEADS, 2 * M_HEAD_DIM, M_HEAD_DIM), F32),
                        pltpu.VMEM((grp, M_HEADS, SUBLANES, LANES), F32)],
        compiler_params=pltpu.CompilerParams(dimension_semantics=("arbitrary", "arbitrary"),
                                             vmem_limit_bytes=VMEM_LIMIT),
    )(seqs(qm), seqs(km), seqs(vm), seqs(so), seqs(gates), nw)
    return out.reshape(bsz * seq, M_WIDTH)


def _attn_kernel(ii_ref, jj_ref, q_ref, k_ref, v_ref, lam_ref, sw_ref, nw_ref, out_ref,
                 vt_ref, kp_ref, qa_ref, acc_ref, l_ref):
    seq = q_ref.shape[0]
    t = A_BLOCK
    nq = seq // t
    lp = lam_ref[...]
    lam = (jnp.exp(jnp.sum(lp[0:1] * lp[1:2], axis=-1, keepdims=True))
           - jnp.exp(jnp.sum(lp[2:3] * lp[3:4], axis=-1, keepdims=True)) + LAM_INIT)
    for jb in range(nq):
        vt_ref[jb] = v_ref[jb * t:(jb + 1) * t, :].T

    nw = jnp.max(jnp.abs(nw_ref[...]), axis=-1, keepdims=True)
    bound = (LOG2_E * D_HEAD_DIM ** 0.5 * BOUND_SLACK) * nw[0:1] * nw[1:2]

    aux = (D_HEAD_DIM, 0)
    k = k_ref[...]
    q = q_ref[...]
    lane_s = lax.broadcasted_iota(jnp.int32, (seq, LANES), 1)
    in_map_s = (lane_s < D_HEAD_DIM, lane_s >= D_HEAD_DIM)
    for mp in range(2):
        kp_ref[mp] = jnp.where(in_map_s[mp], k, jnp.where(lane_s == aux[mp], 1.0, 0.0).astype(BF16))
        qa_ref[mp] = jnp.where(in_map_s[mp], q, jnp.where(lane_s == aux[mp], -bound, 0.0).astype(BF16))
    l_ref[...] = jnp.zeros(l_ref.shape, F32)
    acc_ref[...] = jnp.zeros(acc_ref.shape, F32)

    k_i = lax.broadcasted_iota(jnp.int32, (t, t), 0)
    q_i = lax.broadcasted_iota(jnp.int32, (t, t), 1)
    diag_mask = k_i <= q_i

    def step(i, j, masked):
        koff = pl.multiple_of(j * t, t)
        qoff = pl.multiple_of(i * t, t)
        vt = vt_ref[j]
        for mp in range(2):
            s = _dot_nt(kp_ref[mp, pl.ds(koff, t), :], qa_ref[mp, pl.ds(qoff, t), :])
            if masked:
                s = jnp.where(diag_mask, s, NEG_INF)
            p = jnp.exp2(s)
            l_ref[mp * nq + i] += jnp.sum(p, axis=0, keepdims=True)
            acc_ref[mp * nq + i] += _dot(vt, p.astype(BF16))

    def run(n_steps, step_at):
        trips = n_steps // A_UNROLL

        def body(u, c):
            for r in range(A_UNROLL):
                step_at(u * A_UNROLL + r)
            return c

        if trips:
            lax.fori_loop(0, trips, body, 0)
        for p in range(trips * A_UNROLL, n_steps):
            step_at(p)

    run(nq * (nq - 1) // 2, lambda p: step(ii_ref[p], jj_ref[p], False))
    run(nq, lambda p: step(p, p, True))

    def finish(i, c):
        o_t = (acc_ref[i] * (1.0 / l_ref[i]) - lam * (acc_ref[nq + i] * (1.0 / l_ref[nq + i])))
        nrm = lax.rsqrt(jnp.mean(o_t * o_t, axis=0, keepdims=True) + EPS)
        on = (o_t * nrm).T * sw_ref[...] * (1.0 - LAM_INIT)
        out_ref[pl.ds(pl.multiple_of(i * t, t), t), :] = on.astype(BF16)
        return c

    lax.fori_loop(0, nq, finish, 0)


def _attn_call(qd, kd, vd, lam_rows, subw, qk_nw, bsz, seq):
    t = A_BLOCK
    nq = seq // t
    pairs = [(i, j) for i in range(nq) for j in range(i)]
    ii = jnp.asarray([p[0] for p in pairs] or [0], jnp.int32)
    jj = jnp.asarray([p[1] for p in pairs] or [0], jnp.int32)
    blk = pl.BlockSpec((seq, LANES), lambda b, h, *_: (b, h))
    const = lambda shape: pl.BlockSpec(shape, lambda b, h, *_: (0,) * len(shape))
    return pl.pallas_call(
        _attn_kernel,
        grid_spec=pltpu.PrefetchScalarGridSpec(
            num_scalar_prefetch=2,
            grid=(bsz, D_HEADS),
            in_specs=[blk, blk, blk, const(lam_rows.shape), const(subw.shape), const(qk_nw.shape)],
            out_specs=blk,
            scratch_shapes=[pltpu.VMEM((nq, D_V_DIM, t), BF16),
                            pltpu.VMEM((2, seq, LANES), BF16),
                            pltpu.VMEM((2, seq, LANES), BF16),
                            pltpu.VMEM((2 * nq, D_V_DIM, t), F32),
                            pltpu.VMEM((2 * nq, 1, t), F32)]),
        out_shape=jax.ShapeDtypeStruct((bsz * seq, D_WIDTH), BF16),
        compiler_params=pltpu.CompilerParams(dimension_semantics=("arbitrary", "arbitrary"),
                                             vmem_limit_bytes=VMEM_LIMIT),
    )(ii, jj, qd, kd, vd, lam_rows, subw, qk_nw)


def _merge_kernel(tiles_per_seq, x_ref, hm_ref, hd_ref, sgm_ref, sgd_ref, mod_ref, wbm_ref, wbd_ref, wout_ref,
                  n2w_ref, wr_hi_ref, wr_lo_ref, br_ref, x1_ref, h2_ref, dest_ref, rw_ref, cnt_ref, carry_ref):
    gate1 = mod_ref[0, 2:3, :]
    shift2 = mod_ref[0, 3:4, :]
    scale2 = mod_ref[0, 4:5, :]
    merged = (sgm_ref[...].astype(F32) * _dot(hm_ref[...], wbm_ref[...])
              + sgd_ref[...].astype(F32) * _dot(hd_ref[...], wbd_ref[...]))
    x1 = x_ref[...] + gate1 * _dot(merged.astype(BF16), wout_ref[...])
    ms = jnp.mean(x1 * x1, axis=-1, keepdims=True)
    h2 = (x1 * lax.rsqrt(ms + EPS) * n2w_ref[...]) * (1.0 + scale2) + shift2
    tm = x1.shape[0]
    for ch in range(D_MODEL // LANES):
        cols = slice(ch * LANES, (ch + 1) * LANES)
        x1_ref[pl.ds(ch, tm, stride=SUBLANES), :] = x1[:, cols]
        h2_ref[pl.ds(ch, tm, stride=SUBLANES), :] = h2[:, cols]
    h2_hi = h2.astype(BF16)
    h2_lo = (h2 - h2_hi.astype(F32)).astype(BF16)
    logits = (_dot(h2_hi, wr_hi_ref[...]) + _dot(h2_lo, wr_hi_ref[...]) + _dot(h2_hi, wr_lo_ref[...])
              + br_ref[...])
    lane = lax.broadcasted_iota(jnp.int32, logits.shape, 1)
    lane_f = lane.astype(F32)
    big = float(LANES)

    def top(vals):
        mx = jnp.max(vals, axis=-1, keepdims=True)
        idx = jnp.min(jnp.where(vals == mx, lane_f, big), axis=-1, keepdims=True)
        return mx, idx

    is_group = lane < N_GROUPS
    gmax, gidx = top(jnp.where(is_group, logits, NEG_INF))
    gsum = jnp.sum(jnp.where(is_group, jnp.exp(logits - gmax), 0.0), axis=-1, keepdims=True)
    pg_top = 1.0 / gsum
    lo = N_GROUPS + EXPERTS_PER_GROUP * gidx
    in_group = (lane_f >= lo) & (lane_f < lo + EXPERTS_PER_GROUP)
    el = jnp.where(in_group, logits, NEG_INF)
    e1max, e1idx = top(el)
    e2max, e2idx = top(jnp.where(lane_f == e1idx, NEG_INF, el))
    p2 = jnp.exp(e2max - e1max)
    w1 = pg_top / (1.0 + p2)
    w2 = w1 * p2

    @pl.when(pl.program_id(0) % tiles_per_seq == 0)
    def _():
        carry_ref[...] = jnp.zeros(carry_ref.shape, F32)

    is_e1 = lane_f == e1idx
    is_e2 = lane_f == e2idx
    onehot = jnp.where(is_e1 | is_e2, 1.0, 0.0)
    r_i = lax.broadcasted_iota(jnp.int32, (tm, tm), 0)
    c_i = lax.broadcasted_iota(jnp.int32, (tm, tm), 1)
    before = jnp.where(c_i < r_i, 1.0, 0.0).astype(BF16)
    cum = _dot(before, onehot.astype(BF16)) + carry_ref[...]
    r1 = jnp.sum(jnp.where(is_e1, cum, 0.0), axis=-1, keepdims=True)
    r2 = jnp.sum(jnp.where(is_e2, cum, 0.0), axis=-1, keepdims=True)
    carry = carry_ref[...] + jnp.sum(onehot, axis=0, keepdims=True)
    carry_ref[...] = carry
    cnt_ref[0] = carry

    seq_len = float(tiles_per_seq * tm)
    pos = jnp.where(lane == 0, (e1idx - N_GROUPS) * seq_len + r1, (e2idx - N_GROUPS) * seq_len + r2)
    dest_ref[0] = pos.T[0:SUBLANES, :].astype(jnp.int32)
    rw_ref[0] = jnp.where(lane == 0, w1, w2).T[0:SUBLANES, :]


def _merge_call(x2, hm, hd, sgm, sgd, mod3, wbm, wbd, wout, n2w, wr_hi, wr_lo, br, seq):
    n, d = x2.shape
    tm = TM_MERGE
    tiles_per_seq = seq // tm
    row = lambda w: pl.BlockSpec((tm, w), lambda i: (i, 0))
    consts = [wbm, wbd, wout, n2w, wr_hi, wr_lo, br]
    per_seq = pl.BlockSpec((1, SUBLANES, tm), lambda i: (i // tiles_per_seq, 0, i % tiles_per_seq))
    return pl.pallas_call(
        functools.partial(_merge_kernel, tiles_per_seq),
        grid=(n // tm,),
        in_specs=[row(d), row(M_WIDTH), row(D_WIDTH), row(d), row(d),
                  pl.BlockSpec((1, 6, d), lambda i: (i // tiles_per_seq, 0, 0))]
                 + [_const_spec(a.shape) for a in consts],
        out_specs=[pl.BlockSpec((tm * SUBLANES, LANES), lambda i: (i, 0)),
                   pl.BlockSpec((tm * SUBLANES, LANES), lambda i: (i, 0)),
                   per_seq, per_seq, pl.BlockSpec((1, 1, LANES), lambda i: (i, 0, 0))],
        out_shape=[jax.ShapeDtypeStruct((n * SUBLANES, LANES), F32), jax.ShapeDtypeStruct((n * SUBLANES, LANES), F32),
                   jax.ShapeDtypeStruct((n // seq, SUBLANES, seq), jnp.int32),
                   jax.ShapeDtypeStruct((n // seq, SUBLANES, seq), F32),
                   jax.ShapeDtypeStruct((n // tm, 1, LANES), F32)],
        scratch_shapes=[pltpu.VMEM((1, LANES), F32)],
        compiler_params=pltpu.CompilerParams(dimension_semantics=("arbitrary",),
                                             vmem_limit_bytes=VMEM_LIMIT),
    )(x2, hm, hd, sgm, sgd, mod3, *consts)


def _moe_kernel(dest_hbm, rw_hbm, meta_hbm, h2_hbm, x1_hbm, mod_ref, wgu_hbm, wd_hbm, out_hbm,
                dest, rw, meta, inv, h2buf, acc, wgu_buf, wd_buf, xst, yst, ost, sems):
    b = pl.program_id(0)
    t_blk = h2buf.shape[0] // SUBLANES - 1
    spare = 2 * t_blk
    r = MOE_R
    tile = lambda ref, tok: ref.at[pl.ds(pl.multiple_of(tok * SUBLANES, SUBLANES), SUBLANES), :]

    def weight_copies(e, slot):
        return (pltpu.make_async_copy(wgu_hbm.at[e], wgu_buf.at[slot], sems.at[5 + slot]),
                pltpu.make_async_copy(wd_hbm.at[e], wd_buf.at[slot], sems.at[7 + slot]))

    route_copies = (pltpu.make_async_copy(dest_hbm.at[b], dest, sems.at[0]),
                    pltpu.make_async_copy(rw_hbm.at[b], rw, sems.at[1]),
                    pltpu.make_async_copy(meta_hbm.at[b], meta, sems.at[2]))
    h2_copy = pltpu.make_async_copy(h2_hbm.at[b], h2buf.at[pl.ds(0, t_blk * SUBLANES)], sems.at[3])
    x1_copy = pltpu.make_async_copy(x1_hbm.at[b], acc.at[pl.ds(0, t_blk * SUBLANES)], sems.at[4])

    for cp in route_copies + (h2_copy, x1_copy) + weight_copies(0, 0):
        cp.start()
    xst[...] = jnp.zeros(xst.shape, F32)
    yst[...] = jnp.zeros(yst.shape, F32)
    tile(h2buf, t_blk)[...] = jnp.zeros((SUBLANES, LANES), F32)
    tile(acc, t_blk)[...] = jnp.zeros((SUBLANES, LANES), F32)
    for cp in route_copies:
        cp.wait()

    last = inv.shape[0] - 1

    def pad_body(e, c):
        for j in range(MOE_U - 1):
            inv[jnp.minimum(e * t_blk + meta[e] + j, last)] = spare
        return c

    lax.fori_loop(0, N_EXPERTS, pad_body, 0)

    def inv_body(u, c):
        for j in range(MOE_U):
            s = u * MOE_U + j
            inv[dest[s]] = s
        return c

    lax.fori_loop(0, spare // MOE_U, inv_body, 0)
    h2_copy.wait()
    x1_copy.wait()
    gate2 = mod_ref[0, 5:6, :]
    n_chunks = D_MODEL // LANES

    def expert_body(e, c):
        slot = e % 2
        for cp in weight_copies(e, slot):
            cp.wait()

        @pl.when(e + 1 < N_EXPERTS)
        def _():
            for cp in weight_copies(e + 1, 1 - slot):
                cp.start()

        seg = e * t_blk
        n = meta[e]

        def tile_body(k, c2):
            base = seg + k * r
            groups = (jnp.minimum(n - k * r, r) + MOE_U - 1) // MOE_U

            def gather(u, c3):
                row0 = u * MOE_U
                pos0 = base + row0
                for j in range(MOE_U):
                    xst[pl.ds(row0 + j, SUBLANES, stride=MOE_RS), :] = tile(h2buf, inv[pos0 + j] >> 1)[...]
                return c3

            lax.fori_loop(0, groups, gather, 0)
            x = jnp.concatenate([xst[ch * MOE_RS:ch * MOE_RS + r, :] for ch in range(n_chunks)], axis=1)
            gu = _dot(x.astype(BF16), wgu_buf[slot])
            g = gu[:, :D_EXPERT]
            act = (g * _sigmoid(g) * gu[:, D_EXPERT:]).astype(BF16)
            y = _dot(act, wd_buf[slot]) * gate2
            for ch in range(n_chunks):
                yst[ch * MOE_RS:ch * MOE_RS + r, :] = y[:, ch * LANES:(ch + 1) * LANES]

            def scatter(u, c3):
                toks, vals = [], []
                row0 = u * MOE_U
                pos0 = base + row0
                for j in range(MOE_U):
                    v = inv[pos0 + j]
                    toks.append(v >> 1)
                    vals.append(tile(acc, v >> 1)[...]
                                + rw[v] * yst[pl.ds(row0 + j, SUBLANES, stride=MOE_RS), :])
                for tok, val in zip(toks, vals):
                    tile(acc, tok)[...] = val
                return c3

            lax.fori_loop(0, groups, scatter, 0)
            return c2

        lax.fori_loop(0, (n + r - 1) // r, tile_body, 0)
        return c

    lax.fori_loop(0, N_EXPERTS, expert_body, 0)

    rows = ost.shape[1]
    n_pieces = t_blk // rows

    def out_copy(p):
        return pltpu.make_async_copy(ost.at[p % 2], out_hbm.at[b, pl.ds(p * rows, rows)], sems.at[9 + p % 2])

    for p in range(n_pieces):
        if p >= 2:
            out_copy(p - 2).wait()
        for ch in range(n_chunks):
            ost[p % 2, :, ch * LANES:(ch + 1) * LANES] = acc[pl.ds(p * rows * SUBLANES + ch, rows, stride=SUBLANES), :]
        out_copy(p).start()
    for p in range(max(0, n_pieces - 2), n_pieces):
        out_copy(p).wait()


def _moe_call(dest, rw, meta, h2, x1, mod3, wgu, wd):
    nb = h2.shape[0]
    t_blk = h2.shape[1] // SUBLANES
    d = D_MODEL
    tok_rows = (t_blk + 1) * SUBLANES
    any_spec = pl.BlockSpec(memory_space=pl.ANY)
    return pl.pallas_call(
        _moe_kernel,
        grid=(nb,),
        in_specs=[any_spec, any_spec, any_spec, any_spec, any_spec,
                  pl.BlockSpec((1, 6, d), lambda i: (i, 0, 0)), any_spec, any_spec],
        out_specs=any_spec,
        out_shape=jax.ShapeDtypeStruct((nb, t_blk, d), F32),
        scratch_shapes=[pltpu.SMEM(dest.shape[1:], jnp.int32), pltpu.SMEM(rw.shape[1:], F32),
                        pltpu.SMEM(meta.shape[1:], jnp.int32), pltpu.SMEM((N_EXPERTS * t_blk,), jnp.int32),
                        pltpu.VMEM((tok_rows, LANES), F32), pltpu.VMEM((tok_rows, LANES), F32),
                        pltpu.VMEM((2, d, 2 * D_EXPERT), BF16), pltpu.VMEM((2, D_EXPERT, d), BF16),
                        pltpu.VMEM((SUBLANES * MOE_RS, LANES), F32), pltpu.VMEM((SUBLANES * MOE_RS, LANES), F32),
                        pltpu.VMEM((2, min(MOE_OUT_ROWS, t_blk), d), F32),
                        pltpu.SemaphoreType.DMA((11,))],
        compiler_params=pltpu.CompilerParams(dimension_semantics=("arbitrary",),
                                             vmem_limit_bytes=VMEM_LIMIT),
    )(dest, rw, meta, h2, x1, mod3, wgu, wd)


def _rope_freq_row():
    inv = ROPE_THETA ** (-np.arange(0, ROPE_DIM, 2, dtype=np.float32) / ROPE_DIM)
    row = np.zeros((LANES,), np.float32)
    half = ROPE_DIM // 2
    for base in range(0, LANES, D_HEAD_DIM):
        row[base:base + half] = inv
        row[base + half:base + ROPE_DIM] = inv
    return jnp.asarray(row.reshape(1, LANES))


def _block_diag_ones():
    idx = np.arange(MXU_DIM) // D_HEAD_DIM
    return jnp.asarray((idx[:, None] == idx[None, :]).astype(np.float32), dtype=BF16)


def kernel(x, c, positions, w_ada, b_ada, norm1_w, w_in, b_igate, b_fgate, conv_w, conv_b, mlstm_norm_w, q_norm_w, k_norm_w, lam_q1, lam_k1, lam_q2, lam_k2, subln_w, w_br_m, w_br_d, w_out, norm2_w, w_rg, b_rg, w_re, b_re, w_gate, w_up, w_down):
    bsz, seq, d = x.shape
    n = bsz * seq
    x2 = x.reshape(n, d)
    pos2 = positions.reshape(n, 1)
    l = 0

    mod3 = _ada_call(c, w_ada[l], b_ada[l]).reshape(bsz, 6, d)

    wi = w_in[l]
    gate_lo = 4 * M_WIDTH
    gate_hi = gate_lo + 2 * M_HEADS
    w_packed = jnp.concatenate(
        [wi[:, :gate_lo].astype(BF16),
         jnp.pad(wi[:, gate_lo:gate_hi], ((0, 0), (0, LANES - 2 * M_HEADS))).astype(BF16),
         wi[:, gate_hi:].astype(BF16)], axis=1)
    gate_bias = jnp.pad(jnp.concatenate([b_igate[l], b_fgate[l]]), (0, LANES - 2 * M_HEADS)).reshape(1, LANES)
    reps = D_WIDTH // D_HEAD_DIM
    smalls = [conv_w[l], conv_b[l].reshape(1, -1), gate_bias,
              jnp.tile(q_norm_w[l], reps).reshape(1, D_WIDTH), jnp.tile(k_norm_w[l], reps).reshape(1, D_WIDTH),
              _rope_freq_row(), _block_diag_ones()]
    qm, km, vm, so, gates, qd, kd, vd, sgm, sgd = _in_call(
        x2, pos2, mod3, norm1_w[l].reshape(1, d), w_packed, smalls, seq)

    hm = _mlstm_call(qm, km, vm, so, gates, mlstm_norm_w[l].reshape(1, M_WIDTH), bsz, seq)

    lam_rows = jnp.stack([lam_q1[l], lam_k1[l], lam_q2[l], lam_k2[l]])
    hd = _attn_call(qd, kd, vd, lam_rows, subln_w[l].reshape(1, D_V_DIM), jnp.stack([q_norm_w[l], k_norm_w[l]]),
                    bsz, seq)

    w_r = jnp.pad(jnp.concatenate([w_rg[l], w_re[l]], axis=1), ((0, 0), (0, LANES - N_GROUPS - N_EXPERTS)))
    wr_hi = w_r.astype(BF16)
    wr_lo = (w_r - wr_hi.astype(F32)).astype(BF16)
    b_r = jnp.pad(jnp.concatenate([b_rg[l], b_re[l]]), (0, LANES - N_GROUPS - N_EXPERTS)).reshape(1, LANES)
    x1, h2, dest, rw, cnt = _merge_call(x2, hm, hd, sgm, sgd, mod3, w_br_m[l].astype(BF16), w_br_d[l].astype(BF16),
                                        w_out[l].astype(BF16), norm2_w[l].reshape(1, d), wr_hi, wr_lo, b_r, seq)

    tiles_per_seq = seq // TM_MERGE
    meta = cnt[tiles_per_seq - 1::tiles_per_seq, 0, N_GROUPS:N_GROUPS + N_EXPERTS].astype(jnp.int32)
    meta = jnp.pad(meta, ((0, 0), (0, LANES - N_EXPERTS)))
    interleave = lambda a: a[:, 0:2, :].transpose(0, 2, 1).reshape(bsz, 2 * seq)
    dest = interleave(dest)
    rw = jnp.pad(interleave(rw), ((0, 0), (0, LANES)))
    wgu = jnp.concatenate([w_gate[l], w_up[l]], axis=-1).astype(BF16)
    tok_tiles = (bsz, seq * SUBLANES, LANES)
    return _moe_call(dest, rw, meta, h2.reshape(tok_tiles), x1.reshape(tok_tiles), mod3, wgu,
                     w_down[l].astype(BF16))
```

```python
import functools
import math

import numpy as np
import jax
import jax.numpy as jnp
from jax import lax
from jax.experimental import pallas as pl
from jax.experimental.pallas import tpu as pltpu

F32 = jnp.float32
BF16 = jnp.bfloat16

D_MODEL = 1024
M_HEADS = 4
M_HEAD_DIM = 128
M_WIDTH = M_HEADS * M_HEAD_DIM
CONV_WIDTH = 4
D_HEADS = 4
D_HEAD_DIM = 64
D_V_DIM = 2 * D_HEAD_DIM
D_WIDTH = D_HEADS * D_V_DIM
ROPE_THETA = 500000.0
ROPE_DIM = D_HEAD_DIM // 4
N_GROUPS = 4
EXPERTS_PER_GROUP = 8
N_EXPERTS = N_GROUPS * EXPERTS_PER_GROUP
D_EXPERT = D_MODEL // 4
EPS = 1e-6
LAM_INIT = 0.8 - 0.6 * math.exp(-0.3 * 0)

LANES = 128
SUBLANES = 8
MXU_DIM = 256
VMEM_LIMIT = 56 * 1024 * 1024

TM_IN = 512
IN_SPLIT = 2
MERGE_SPLIT = 1
ROUTE_ROWS = 40
TM_MERGE = 512
M_CHUNK = 256
M_GROUP = 4
A_BLOCK = 512
LOG2_E = math.log2(math.e)
A_UNROLL = 4
BOUND_SLACK = 1.01
MOE_R = 256
MOE_RS = MOE_R + SUBLANES
MOE_U = 16
MOE_OUT_ROWS = 512
MOE_SPLIT = 1
IN_SEGMENTS = (("qk", 2 * M_WIDTH), ("v", M_WIDTH), ("o", M_WIDTH), ("g", LANES),
               ("qd", D_WIDTH), ("kd", D_WIDTH), ("vd", D_WIDTH), ("gm", D_MODEL), ("gd", D_MODEL))
NEG_INF = float("-inf")


def _dot(a, b):
    return jnp.dot(a, b, preferred_element_type=F32)


def _dot_nt(a, b):
    return lax.dot_general(a, b, (((1,), (1,)), ((), ())), preferred_element_type=F32)


def _dot_tn(a, b):
    return lax.dot_general(a, b, (((0,), (0,)), ((), ())), preferred_element_type=F32)


def _sigmoid(v):
    return 0.5 * jnp.tanh(0.5 * v) + 0.5


def _const_spec(shape):
    nd = len(shape)
    return pl.BlockSpec(shape, lambda *_: (0,) * nd)


def _ada_kernel(c_ref, w_ref, b_ref, o_ref):
    o_ref[...] = _dot(c_ref[...].astype(BF16), w_ref[...].astype(BF16)) + b_ref[...]


def _ada_call(c, w_ada, b_ada):
    bsz, d = c.shape
    cols = w_ada.shape[1]
    tn = 1024
    return pl.pallas_call(
        _ada_kernel,
        grid=(cols // tn,),
        in_specs=[pl.BlockSpec((bsz, d), lambda j: (0, 0)),
                  pl.BlockSpec((d, tn), lambda j: (0, j)),
                  pl.BlockSpec((1, tn), lambda j: (0, j))],
        out_specs=pl.BlockSpec((bsz, tn), lambda j: (0, j)),
        out_shape=jax.ShapeDtypeStruct((bsz, cols), F32),
        compiler_params=pltpu.CompilerParams(dimension_semantics=("arbitrary",)),
    )(c, w_ada, b_ada.reshape(1, cols))


def _in_kernel(tiles_per_seq,
               x_ref, pos_ref, mod_ref, n1w_ref,
               w_ref, cw_ref, cb_ref, gb_ref, qnw_ref, knw_ref, freq_ref, bd_ref,
               qm_ref, km_ref, vm_ref, so_ref, gates_ref, qd_ref, kd_ref, vd_ref, sgm_ref, sgd_ref,
               cbuf):
    tm = x_ref.shape[0]
    i = pl.program_id(0)
    seg, o = {}, 0
    for name, width in IN_SEGMENTS:
        seg[name] = w_ref.at[:, o:o + width]
        o += width

    @pl.when(i == 0)
    def _():
        cbuf[...] = jnp.zeros(cbuf.shape, F32)

    prev = jnp.where(i % tiles_per_seq == 0, 0.0, cbuf[...])
    rows = tm // IN_SPLIT
    for part in range(IN_SPLIT):
        prev = _in_rows(slice(part * rows, (part + 1) * rows), prev,
                        x_ref, pos_ref, mod_ref, n1w_ref, seg, cw_ref, cb_ref, gb_ref, qnw_ref, knw_ref, freq_ref,
                        bd_ref, qm_ref, km_ref, vm_ref, so_ref, gates_ref, qd_ref, kd_ref, vd_ref, sgm_ref, sgd_ref)
    cbuf[...] = prev


def _in_rows(rs, prev, x_ref, pos_ref, mod_ref, n1w_ref, seg, cw_ref, cb_ref, gb_ref, qnw_ref, knw_ref, freq_ref,
             bd_ref, qm_ref, km_ref, vm_ref, so_ref, gates_ref, qd_ref, kd_ref, vd_ref, sgm_ref, sgd_ref):
    wqk_ref, wv_ref, wo_ref, wg_ref = seg["qk"], seg["v"], seg["o"], seg["g"]
    wqd_ref, wkd_ref, wvd_ref, wgm_ref, wgd_ref = seg["qd"], seg["kd"], seg["vd"], seg["gm"], seg["gd"]
    x = x_ref[rs, :]
    tm = x.shape[0]
    ms = jnp.mean(x * x, axis=-1, keepdims=True)
    scale1 = mod_ref[0, 1:2, :]
    shift1 = mod_ref[0, 0:1, :]
    h = (x * lax.rsqrt(ms + EPS) * n1w_ref[...]) * (1.0 + scale1) + shift1
    hb = h.astype(BF16)

    y = _dot(hb, wqk_ref[...])
    row8 = lax.broadcasted_iota(jnp.int32, (SUBLANES, 2 * M_WIDTH), 0)
    acc = cb_ref[...] + cw_ref[3:4, :] * y
    for d in range(1, CONV_WIDTH):
        shifted = pltpu.roll(y, d, 0)
        head = jnp.where(row8 < d, pltpu.roll(prev, d, 0), shifted[:SUBLANES])
        shifted = jnp.concatenate([head, shifted[SUBLANES:]], axis=0)
        acc = acc + cw_ref[3 - d:4 - d, :] * shifted
    a = acc * _sigmoid(acc)
    qm_ref[rs, :] = a[:, :M_WIDTH].astype(BF16)
    km_ref[rs, :] = (a[:, M_WIDTH:] * (M_HEAD_DIM ** -0.5)).astype(BF16)

    sgm_ref[rs, :] = _sigmoid(_dot(hb, wgm_ref[...])).astype(BF16)
    vm_ref[rs, :] = _dot(hb, wv_ref[...]).astype(BF16)
    so_ref[rs, :] = _sigmoid(_dot(hb, wo_ref[...])).astype(BF16)

    g = _dot(hb, wg_ref[...]) + gb_ref[...]
    lane = lax.broadcasted_iota(jnp.int32, g.shape, 1)
    logsig = jnp.minimum(g, 0.0) - jnp.log(1.0 + jnp.exp(-jnp.abs(g)))
    gates_ref[rs, :] = jnp.where(lane < M_HEADS, g, logsig)

    ang = pos_ref[rs, :].astype(F32) * freq_ref[...]
    cos = jnp.cos(ang)
    sin = jnp.sin(ang)
    lane_in_map = lax.broadcasted_iota(jnp.int32, ang.shape, 1) % D_HEAD_DIM
    half = ROPE_DIM // 2
    s_from_lo = jnp.where(lane_in_map >= half, sin, 0.0)
    s_from_hi = jnp.where(lane_in_map < half, -sin, 0.0)
    reps = D_WIDTH // LANES
    cos_w = jnp.concatenate([cos] * reps, axis=1)
    slo_w = jnp.concatenate([s_from_lo] * reps, axis=1)
    shi_w = jnp.concatenate([s_from_hi] * reps, axis=1)

    def qk_norm_rope(w_ref, nw_ref, out_scale):
        t = _dot(hb, w_ref[...])
        sq = (t * t).astype(BF16)
        gs = [_dot(sq[:, c * MXU_DIM:(c + 1) * MXU_DIM], bd_ref[...]) for c in range(D_WIDTH // MXU_DIM)]
        msq = jnp.concatenate(gs, axis=1) * (1.0 / D_HEAD_DIM)
        tn = t * lax.rsqrt(msq + EPS) * nw_ref[...]
        rot = (tn * cos_w + pltpu.roll(tn, half, 1) * slo_w
               + pltpu.roll(tn, D_WIDTH - half, 1) * shi_w)
        return (rot * out_scale).astype(BF16)

    qd_ref[rs, :] = qk_norm_rope(wqd_ref, qnw_ref, D_HEAD_DIM ** -0.5 * LOG2_E)
    sgd_ref[rs, :] = _sigmoid(_dot(hb, wgd_ref[...])).astype(BF16)
    kd_ref[rs, :] = qk_norm_rope(wkd_ref, knw_ref, 1.0)
    vd_ref[rs, :] = _dot(hb, wvd_ref[...]).astype(BF16)
    return y[tm - SUBLANES:, :]


def _in_call(x2, pos2, mod3, n1w, w_packed, smalls, seq):
    n, d = x2.shape
    tm = TM_IN
    tiles_per_seq = seq // tm
    row = lambda w: pl.BlockSpec((tm, w), lambda i: (i, 0))
    in_specs = [row(d), row(1),
                pl.BlockSpec((1, 6, d), lambda i: (i // tiles_per_seq, 0, 0)),
                _const_spec(n1w.shape), _const_spec(w_packed.shape)]
    in_specs += [_const_spec(s.shape) for s in smalls]
    out_widths = [(M_WIDTH, BF16), (M_WIDTH, BF16), (M_WIDTH, BF16), (M_WIDTH, BF16), (LANES, F32),
                  (D_WIDTH, BF16), (D_WIDTH, BF16), (D_WIDTH, BF16), (d, BF16), (d, BF16)]
    return pl.pallas_call(
        functools.partial(_in_kernel, tiles_per_seq),
        grid=(n // tm,),
        in_specs=in_specs,
        out_specs=[row(w) for w, _ in out_widths],
        out_shape=[jax.ShapeDtypeStruct((n, w), dt) for w, dt in out_widths],
        scratch_shapes=[pltpu.VMEM((SUBLANES, 2 * M_WIDTH), F32)],
        compiler_params=pltpu.CompilerParams(dimension_semantics=("arbitrary",),
                                             vmem_limit_bytes=VMEM_LIMIT),
    )(x2, pos2, mod3, n1w, w_packed, *smalls)


def _mlstm_kernel(grp, q_ref, k_ref, v_ref, so_ref, gates_ref, nw_ref, out_ref, cx_ref, m_ref):
    @pl.when(pl.program_id(1) == 0)
    def _():
        cx_ref[...] = jnp.zeros(cx_ref.shape, F32)
        m_ref[...] = jnp.zeros(m_ref.shape, F32)

    for gi in range(grp):
        _mlstm_chunk(q_ref.at[gi], k_ref.at[gi], v_ref.at[gi], so_ref.at[gi], gates_ref.at[gi], nw_ref,
                     out_ref.at[gi], cx_ref.at[gi], m_ref.at[gi])


def _mlstm_chunk(q_ref, k_ref, v_ref, so_ref, gates_ref, nw_ref, out_ref, cx_ref, m_ref):
    L = q_ref.shape[0]
    hd = M_HEAD_DIM
    g = gates_ref[...]
    r_i = lax.broadcasted_iota(jnp.int32, (L, L), 0)
    c_i = lax.broadcasted_iota(jnp.int32, (L, L), 1)
    causal = c_i <= r_i
    tri = causal.astype(BF16)
    g_hi = g.astype(BF16)
    g_mid = (g - g_hi.astype(F32)).astype(BF16)
    g_lo = (g - g_hi.astype(F32) - g_mid.astype(F32)).astype(BF16)
    bc = _dot(tri, g_hi) + _dot(tri, g_mid) + _dot(tri, g_lo)
    g_t = g.T
    bc_t = bc.T
    ones_col = jnp.where(lax.broadcasted_iota(jnp.int32, (L, hd), 1) == 0, 1.0, 0.0).astype(BF16)

    causal_t = r_i <= c_i
    for h in range(M_HEADS):
        sl = slice(h * hd, (h + 1) * hd)
        i_row = g_t[h:h + 1, :]
        b_row = bc_t[M_HEADS + h:M_HEADS + h + 1, :]
        r_col = g[:, h:h + 1] - bc[:, M_HEADS + h:M_HEADS + h + 1]
        m_prev = m_ref[h][0:1, 0:1]
        cxt = cx_ref[h]

        d_t = jnp.where(causal_t, b_row + r_col, NEG_INF)
        m_inter = b_row + m_prev
        m_row = jnp.maximum(m_inter, jnp.max(d_t, axis=0, keepdims=True))
        q = q_ref[:, sl]
        k = k_ref[:, sl]
        vx = jnp.concatenate([v_ref[:, sl], ones_col], axis=1)
        s_t = _dot_nt(k, q) * jnp.exp(d_t - m_row)
        inter = jnp.exp(m_inter - m_row)
        num_t = _dot_tn(vx, s_t.astype(BF16)) + inter * _dot_nt(cxt.astype(BF16), q)
        den = jnp.maximum(jnp.abs(num_t[hd:hd + 1, :]), jnp.exp(-m_row))
        h_t = num_t[:hd, :] * (1.0 / den)

        b_last = b_row[:, L - 1:L]
        m_new = jnp.maximum(b_last + m_prev, jnp.max(b_last + (i_row - b_row), axis=1, keepdims=True))
        wexp = jnp.exp(b_last + r_col - m_new)
        decay = jnp.exp(b_last + m_prev - m_new)
        kw = (k.astype(F32) * wexp).astype(BF16)
        cx_ref[h] = decay * cxt + _dot_tn(vx, kw)
        m_ref[h] = jnp.broadcast_to(m_new, m_ref.shape[1:])

        hn_t = h_t * lax.rsqrt(jnp.mean(h_t * h_t, axis=0, keepdims=True) + EPS)
        out_ref[:, sl] = (hn_t.T * nw_ref[:, sl] * so_ref[:, sl].astype(F32)).astype(BF16)


def _mlstm_call(qm, km, vm, so, gates, nw, bsz, seq):
    L = M_CHUNK
    grp = math.gcd(bsz, M_GROUP)
    blk = lambda w: pl.BlockSpec((grp, L, w), lambda b, c: (b, c, 0))
    seqs = lambda a: a.reshape(bsz, seq, a.shape[-1])
    out = pl.pallas_call(
        functools.partial(_mlstm_kernel, grp),
        grid=(bsz // grp, seq // L),
        in_specs=[blk(M_WIDTH), blk(M_WIDTH), blk(M_WIDTH), blk(M_WIDTH), blk(LANES),
                  pl.BlockSpec((1, M_WIDTH), lambda b, c: (0, 0))],
        out_specs=blk(M_WIDTH),
        out_shape=jax.ShapeDtypeStruct((bsz, seq, M_WIDTH), BF16),
        scratch_shapes=[pltpu.VMEM((grp, M_HEADS, 2 * M_HEAD_DIM, M_HEAD_DIM), F32),
                        pltpu.VMEM((grp, M_HEADS, SUBLANES, LANES), F32)],
        compiler_params=pltpu.CompilerParams(dimension_semantics=("arbitrary", "arbitrary"),
                                             vmem_limit_bytes=VMEM_LIMIT),
    )(seqs(qm), seqs(km), seqs(vm), seqs(so), seqs(gates), nw)
    return out.reshape(bsz * seq, M_WIDTH)


def _attn_kernel(ii_ref, jj_ref, q_ref, k_ref, v_ref, lam_ref, sw_ref, nw_ref, out_ref,
                 vt_ref, kp_ref, qa_ref, acc_ref, l_ref):
    seq = q_ref.shape[0]
    t = A_BLOCK
    nq = seq // t
    lp = lam_ref[...]
    lam = (jnp.exp(jnp.sum(lp[0:1] * lp[1:2], axis=-1, keepdims=True))
           - jnp.exp(jnp.sum(lp[2:3] * lp[3:4], axis=-1, keepdims=True)) + LAM_INIT)
    for jb in range(nq):
        vt_ref[jb] = v_ref[jb * t:(jb + 1) * t, :].T

    nw = jnp.max(jnp.abs(nw_ref[...]), axis=-1, keepdims=True)
    bound = (LOG2_E * D_HEAD_DIM ** 0.5 * BOUND_SLACK) * nw[0:1] * nw[1:2]

    aux = (D_HEAD_DIM, 0)
    k = k_ref[...]
    q = q_ref[...]
    lane_s = lax.broadcasted_iota(jnp.int32, (seq, LANES), 1)
    in_map_s = (lane_s < D_HEAD_DIM, lane_s >= D_HEAD_DIM)
    for mp in range(2):
        kp_ref[mp] = jnp.where(in_map_s[mp], k, jnp.where(lane_s == aux[mp], 1.0, 0.0).astype(BF16))
        qa_ref[mp] = jnp.where(in_map_s[mp], q, jnp.where(lane_s == aux[mp], -bound, 0.0).astype(BF16))
    l_ref[...] = jnp.zeros(l_ref.shape, F32)
    acc_ref[...] = jnp.zeros(acc_ref.shape, F32)

    k_i = lax.broadcasted_iota(jnp.int32, (t, t), 0)
    q_i = lax.broadcasted_iota(jnp.int32, (t, t), 1)
    diag_mask = k_i <= q_i

    def step(i, j, masked):
        koff = pl.multiple_of(j * t, t)
        qoff = pl.multiple_of(i * t, t)
        vt = vt_ref[j]
        for mp in range(2):
            s = _dot_nt(kp_ref[mp, pl.ds(koff, t), :], qa_ref[mp, pl.ds(qoff, t), :])
            if masked:
                s = jnp.where(diag_mask, s, NEG_INF)
            p = jnp.exp2(s)
            l_ref[mp * nq + i] += jnp.sum(p, axis=0, keepdims=True)
            acc_ref[mp * nq + i] += _dot(vt, p.astype(BF16))

    def run(n_steps, step_at):
        trips = n_steps // A_UNROLL

        def body(u, c):
            for r in range(A_UNROLL):
                step_at(u * A_UNROLL + r)
            return c

        if trips:
            lax.fori_loop(0, trips, body, 0)
        for p in range(trips * A_UNROLL, n_steps):
            step_at(p)

    run(nq * (nq - 1) // 2, lambda p: step(ii_ref[p], jj_ref[p], False))
    run(nq, lambda p: step(p, p, True))

    def finish(i, c):
        o_t = (acc_ref[i] * (1.0 / l_ref[i]) - lam * (acc_ref[nq + i] * (1.0 / l_ref[nq + i])))
        nrm = lax.rsqrt(jnp.mean(o_t * o_t, axis=0, keepdims=True) + EPS)
        on = (o_t * nrm).T * sw_ref[...] * (1.0 - LAM_INIT)
        out_ref[pl.ds(pl.multiple_of(i * t, t), t), :] = on.astype(BF16)
        return c

    lax.fori_loop(0, nq, finish, 0)


def _attn_call(qd, kd, vd, lam_rows, subw, qk_nw, bsz, seq):
    t = A_BLOCK
    nq = seq // t
    pairs = [(i, j) for i in range(nq) for j in range(i)]
    ii = jnp.asarray([p[0] for p in pairs] or [0], jnp.int32)
    jj = jnp.asarray([p[1] for p in pairs] or [0], jnp.int32)
    blk = pl.BlockSpec((seq, LANES), lambda b, h, *_: (b, h))
    const = lambda shape: pl.BlockSpec(shape, lambda b, h, *_: (0,) * len(shape))
    return pl.pallas_call(
        _attn_kernel,
        grid_spec=pltpu.PrefetchScalarGridSpec(
            num_scalar_prefetch=2,
            grid=(bsz, D_HEADS),
            in_specs=[blk, blk, blk, const(lam_rows.shape), const(subw.shape), const(qk_nw.shape)],
            out_specs=blk,
            scratch_shapes=[pltpu.VMEM((nq, D_V_DIM, t), BF16),
                            pltpu.VMEM((2, seq, LANES), BF16),
                            pltpu.VMEM((2, seq, LANES), BF16),
                            pltpu.VMEM((2 * nq, D_V_DIM, t), F32),
                            pltpu.VMEM((2 * nq, 1, t), F32)]),
        out_shape=jax.ShapeDtypeStruct((bsz * seq, D_WIDTH), BF16),
        compiler_params=pltpu.CompilerParams(dimension_semantics=("arbitrary", "arbitrary"),
                                             vmem_limit_bytes=VMEM_LIMIT),
    )(ii, jj, qd, kd, vd, lam_rows, subw, qk_nw)


def _merge_kernel(tiles_per_seq, x_ref, hm_ref, hd_ref, sgm_ref, sgd_ref, mod_ref, wbm_ref, wbd_ref, wout_ref,
                  n2w_ref, wr_hi_ref, wr_lo_ref, br_ref, x1_ref, h2_ref, dest_ref, rw_ref, cnt_ref, carry_ref):
    gate1 = mod_ref[0, 2:3, :]
    shift2 = mod_ref[0, 3:4, :]
    scale2 = mod_ref[0, 4:5, :]
    tm = x_ref.shape[0]
    rows = tm // MERGE_SPLIT
    parts = []
    for part in range(MERGE_SPLIT):
        rs = slice(part * rows, (part + 1) * rows)
        merged = (sgm_ref[rs, :].astype(F32) * _dot(hm_ref[rs, :], wbm_ref[...])
                  + sgd_ref[rs, :].astype(F32) * _dot(hd_ref[rs, :], wbd_ref[...]))
        x1 = x_ref[rs, :] + gate1 * _dot(merged.astype(BF16), wout_ref[...])
        ms = jnp.mean(x1 * x1, axis=-1, keepdims=True)
        h2 = (x1 * lax.rsqrt(ms + EPS) * n2w_ref[...]) * (1.0 + scale2) + shift2
        for ch in range(D_MODEL // LANES):
            cols = slice(ch * LANES, (ch + 1) * LANES)
            x1_ref[pl.ds(part * rows * SUBLANES + ch, rows, stride=SUBLANES), :] = x1[:, cols]
            h2_ref[pl.ds(part * rows * SUBLANES + ch, rows, stride=SUBLANES), :] = h2[:, cols]
        h2_hi = h2.astype(BF16)
        h2_lo = (h2 - h2_hi.astype(F32)).astype(BF16)
        parts.append(_dot(h2_hi, wr_hi_ref[...]) + _dot(h2_lo, wr_hi_ref[...]) + _dot(h2_hi, wr_lo_ref[...]))
    logits = (jnp.concatenate(parts, axis=0) + br_ref[...]).T[0:ROUTE_ROWS, :]
    row = lax.broadcasted_iota(jnp.int32, logits.shape, 0)
    row_f = row.astype(F32)
    big = float(LANES)

    def top(vals):
        mx = jnp.max(vals, axis=0, keepdims=True)
        idx = jnp.min(jnp.where(vals == mx, row_f, big), axis=0, keepdims=True)
        return mx, idx

    is_group = row < N_GROUPS
    gmax, gidx = top(jnp.where(is_group, logits, NEG_INF))
    gsum = jnp.sum(jnp.where(is_group, jnp.exp(logits - gmax), 0.0), axis=0, keepdims=True)
    pg_top = 1.0 / gsum
    lo = N_GROUPS + EXPERTS_PER_GROUP * gidx
    in_group = (row_f >= lo) & (row_f < lo + EXPERTS_PER_GROUP)
    el = jnp.where(in_group, logits, NEG_INF)
    e1max, e1idx = top(el)
    e2max, e2idx = top(jnp.where(row_f == e1idx, NEG_INF, el))
    p2 = jnp.exp(e2max - e1max)
    w1 = pg_top / (1.0 + p2)
    w2 = w1 * p2

    @pl.when(pl.program_id(0) % tiles_per_seq == 0)
    def _():
        carry_ref[...] = jnp.zeros(carry_ref.shape, F32)

    is_e1 = row_f == e1idx
    is_e2 = row_f == e2idx
    onehot = jnp.where(is_e1 | is_e2, 1.0, 0.0)
    r_i = lax.broadcasted_iota(jnp.int32, (tm, tm), 0)
    c_i = lax.broadcasted_iota(jnp.int32, (tm, tm), 1)
    before = jnp.where(r_i < c_i, 1.0, 0.0).astype(BF16)
    cum = _dot(onehot.astype(BF16), before) + carry_ref[...]
    r1 = jnp.sum(jnp.where(is_e1, cum, 0.0), axis=0, keepdims=True)
    r2 = jnp.sum(jnp.where(is_e2, cum, 0.0), axis=0, keepdims=True)
    carry = carry_ref[...] + jnp.sum(onehot, axis=1, keepdims=True)
    carry_ref[...] = carry
    cnt_ref[0] = carry

    seq_len = float(tiles_per_seq * tm)
    row8 = lax.broadcasted_iota(jnp.int32, (SUBLANES, tm), 0)
    pos = jnp.where(row8 == 0, (e1idx - N_GROUPS) * seq_len + r1, (e2idx - N_GROUPS) * seq_len + r2)
    dest_ref[0] = pos.astype(jnp.int32)
    rw_ref[0] = jnp.where(row8 == 0, w1, w2)


def _merge_call(x2, hm, hd, sgm, sgd, mod3, wbm, wbd, wout, n2w, wr_hi, wr_lo, br, seq):
    n, d = x2.shape
    tm = TM_MERGE
    tiles_per_seq = seq // tm
    row = lambda w: pl.BlockSpec((tm, w), lambda i: (i, 0))
    consts = [wbm, wbd, wout, n2w, wr_hi, wr_lo, br]
    per_seq = pl.BlockSpec((1, SUBLANES, tm), lambda i: (i // tiles_per_seq, 0, i % tiles_per_seq))
    return pl.pallas_call(
        functools.partial(_merge_kernel, tiles_per_seq),
        grid=(n // tm,),
        in_specs=[row(d), row(M_WIDTH), row(D_WIDTH), row(d), row(d),
                  pl.BlockSpec((1, 6, d), lambda i: (i // tiles_per_seq, 0, 0))]
                 + [_const_spec(a.shape) for a in consts],
        out_specs=[pl.BlockSpec((tm * SUBLANES, LANES), lambda i: (i, 0)),
                   pl.BlockSpec((tm * SUBLANES, LANES), lambda i: (i, 0)),
                   per_seq, per_seq, pl.BlockSpec((1, ROUTE_ROWS, 1), lambda i: (i, 0, 0))],
        out_shape=[jax.ShapeDtypeStruct((n * SUBLANES, LANES), F32), jax.ShapeDtypeStruct((n * SUBLANES, LANES), F32),
                   jax.ShapeDtypeStruct((n // seq, SUBLANES, seq), jnp.int32),
                   jax.ShapeDtypeStruct((n // seq, SUBLANES, seq), F32),
                   jax.ShapeDtypeStruct((n // tm, ROUTE_ROWS, 1), F32)],
        scratch_shapes=[pltpu.VMEM((ROUTE_ROWS, 1), F32)],
        compiler_params=pltpu.CompilerParams(dimension_semantics=("arbitrary",),
                                             vmem_limit_bytes=VMEM_LIMIT),
    )(x2, hm, hd, sgm, sgd, mod3, *consts)


def _moe_kernel(dest_hbm, rw_hbm, meta_hbm, h2_hbm, x1_hbm, mod_ref, wgu_hbm, wd_hbm, out_hbm,
                dest, rw, meta, inv, h2buf, acc, wgu_buf, wd_buf, xst, yst, ost, sems):
    b = pl.program_id(0)
    t_blk = h2buf.shape[0] // SUBLANES - 1
    spare = 2 * t_blk
    r = MOE_R
    tile = lambda ref, tok: ref.at[pl.ds(pl.multiple_of(tok * SUBLANES, SUBLANES), SUBLANES), :]

    def weight_copies(e, slot):
        return (pltpu.make_async_copy(wgu_hbm.at[e], wgu_buf.at[slot], sems.at[5 + slot]),
                pltpu.make_async_copy(wd_hbm.at[e], wd_buf.at[slot], sems.at[7 + slot]))

    route_copies = (pltpu.make_async_copy(dest_hbm.at[b], dest, sems.at[0]),
                    pltpu.make_async_copy(rw_hbm.at[b], rw, sems.at[1]),
                    pltpu.make_async_copy(meta_hbm.at[b], meta, sems.at[2]))
    h2_copy = pltpu.make_async_copy(h2_hbm.at[b], h2buf.at[pl.ds(0, t_blk * SUBLANES)], sems.at[3])
    x1_copy = pltpu.make_async_copy(x1_hbm.at[b], acc.at[pl.ds(0, t_blk * SUBLANES)], sems.at[4])

    for cp in route_copies + (h2_copy, x1_copy) + weight_copies(0, 0):
        cp.start()
    xst[...] = jnp.zeros(xst.shape, F32)
    yst[...] = jnp.zeros(yst.shape, F32)
    tile(h2buf, t_blk)[...] = jnp.zeros((SUBLANES, LANES), F32)
    tile(acc, t_blk)[...] = jnp.zeros((SUBLANES, LANES), F32)
    for cp in route_copies:
        cp.wait()

    last = inv.shape[0] - 1

    def pad_body(e, c):
        for j in range(MOE_U - 1):
            inv[jnp.minimum(e * t_blk + meta[e] + j, last)] = spare
        return c

    lax.fori_loop(0, N_EXPERTS, pad_body, 0)

    def inv_body(u, c):
        for j in range(MOE_U):
            s = u * MOE_U + j
            inv[dest[s]] = s
        return c

    lax.fori_loop(0, spare // MOE_U, inv_body, 0)
    h2_copy.wait()
    x1_copy.wait()
    gate2 = mod_ref[0, 5:6, :]
    n_chunks = D_MODEL // LANES

    def expert_body(e, c):
        slot = e % 2
        for cp in weight_copies(e, slot):
            cp.wait()

        @pl.when(e + 1 < N_EXPERTS)
        def _():
            for cp in weight_copies(e + 1, 1 - slot):
                cp.start()

        seg = e * t_blk
        n = meta[e]

        def tile_body(k, c2):
            base = seg + k * r
            groups = (jnp.minimum(n - k * r, r) + MOE_U - 1) // MOE_U

            def gather(u, c3):
                row0 = u * MOE_U
                pos0 = base + row0
                for j in range(MOE_U):
                    xst[pl.ds(row0 + j, SUBLANES, stride=MOE_RS), :] = tile(h2buf, inv[pos0 + j] >> 1)[...]
                return c3

            lax.fori_loop(0, groups, gather, 0)
            for part in range(MOE_SPLIT):
                r0 = part * (r // MOE_SPLIT)
                r1 = r0 + r // MOE_SPLIT
                x = jnp.concatenate([xst[ch * MOE_RS + r0:ch * MOE_RS + r1, :] for ch in range(n_chunks)], axis=1)
                gu = _dot(x.astype(BF16), wgu_buf[slot])
                g = gu[:, :D_EXPERT]
                act = (g * _sigmoid(g) * gu[:, D_EXPERT:]).astype(BF16)
                y = _dot(act, wd_buf[slot]) * gate2
                for ch in range(n_chunks):
                    yst[ch * MOE_RS + r0:ch * MOE_RS + r1, :] = y[:, ch * LANES:(ch + 1) * LANES]

            def scatter(u, c3):
                toks, vals = [], []
                row0 = u * MOE_U
                pos0 = base + row0
                for j in range(MOE_U):
                    v = inv[pos0 + j]
                    toks.append(v >> 1)
                    vals.append(tile(acc, v >> 1)[...]
                                + rw[v] * yst[pl.ds(row0 + j, SUBLANES, stride=MOE_RS), :])
                for tok, val in zip(toks, vals):
                    tile(acc, tok)[...] = val
                return c3

            lax.fori_loop(0, groups, scatter, 0)
            return c2

        lax.fori_loop(0, (n + r - 1) // r, tile_body, 0)
        return c

    lax.fori_loop(0, N_EXPERTS, expert_body, 0)

    rows = ost.shape[1]
    n_pieces = t_blk // rows

    def out_copy(p):
        return pltpu.make_async_copy(ost.at[p % 2], out_hbm.at[b, pl.ds(p * rows, rows)], sems.at[9 + p % 2])

    for p in range(n_pieces):
        if p >= 2:
            out_copy(p - 2).wait()
        for ch in range(n_chunks):
            ost[p % 2, :, ch * LANES:(ch + 1) * LANES] = acc[pl.ds(p * rows * SUBLANES + ch, rows, stride=SUBLANES), :]
        out_copy(p).start()
    for p in range(max(0, n_pieces - 2), n_pieces):
        out_copy(p).wait()


def _moe_call(dest, rw, meta, h2, x1, mod3, wgu, wd):
    nb = h2.shape[0]
    t_blk = h2.shape[1] // SUBLANES
    d = D_MODEL
    tok_rows = (t_blk + 1) * SUBLANES
    any_spec = pl.BlockSpec(memory_space=pl.ANY)
    return pl.pallas_call(
        _moe_kernel,
        grid=(nb,),
        in_specs=[any_spec, any_spec, any_spec, any_spec, any_spec,
                  pl.BlockSpec((1, 6, d), lambda i: (i, 0, 0)), any_spec, any_spec],
        out_specs=any_spec,
        out_shape=jax.ShapeDtypeStruct((nb, t_blk, d), F32),
        scratch_shapes=[pltpu.SMEM(dest.shape[1:], jnp.int32), pltpu.SMEM(rw.shape[1:], F32),
                        pltpu.SMEM(meta.shape[1:], jnp.int32), pltpu.SMEM((N_EXPERTS * t_blk,), jnp.int32),
                        pltpu.VMEM((tok_rows, LANES), F32), pltpu.VMEM((tok_rows, LANES), F32),
                        pltpu.VMEM((2, d, 2 * D_EXPERT), BF16), pltpu.VMEM((2, D_EXPERT, d), BF16),
                        pltpu.VMEM((SUBLANES * MOE_RS, LANES), F32), pltpu.VMEM((SUBLANES * MOE_RS, LANES), F32),
                        pltpu.VMEM((2, min(MOE_OUT_ROWS, t_blk), d), F32),
                        pltpu.SemaphoreType.DMA((11,))],
        compiler_params=pltpu.CompilerParams(dimension_semantics=("arbitrary",),
                                             vmem_limit_bytes=VMEM_LIMIT),
    )(dest, rw, meta, h2, x1, mod3, wgu, wd)


def _rope_freq_row():
    inv = ROPE_THETA ** (-np.arange(0, ROPE_DIM, 2, dtype=np.float32) / ROPE_DIM)
    row = np.zeros((LANES,), np.float32)
    half = ROPE_DIM // 2
    for base in range(0, LANES, D_HEAD_DIM):
        row[base:base + half] = inv
        row[base + half:base + ROPE_DIM] = inv
    return jnp.asarray(row.reshape(1, LANES))


def _block_diag_ones():
    idx = np.arange(MXU_DIM) // D_HEAD_DIM
    return jnp.asarray((idx[:, None] == idx[None, :]).astype(np.float32), dtype=BF16)


def kernel(x, c, positions, w_ada, b_ada, norm1_w, w_in, b_igate, b_fgate, conv_w, conv_b, mlstm_norm_w, q_norm_w, k_norm_w, lam_q1, lam_k1, lam_q2, lam_k2, subln_w, w_br_m, w_br_d, w_out, norm2_w, w_rg, b_rg, w_re, b_re, w_gate, w_up, w_down):
    bsz, seq, d = x.shape
    n = bsz * seq
    x2 = x.reshape(n, d)
    pos2 = positions.reshape(n, 1)
    l = 0

    mod3 = _ada_call(c, w_ada[l], b_ada[l]).reshape(bsz, 6, d)

    wi = w_in[l]
    gate_lo = 4 * M_WIDTH
    gate_hi = gate_lo + 2 * M_HEADS
    w_packed = jnp.concatenate(
        [wi[:, :gate_lo].astype(BF16),
         jnp.pad(wi[:, gate_lo:gate_hi], ((0, 0), (0, LANES - 2 * M_HEADS))).astype(BF16),
         wi[:, gate_hi:].astype(BF16)], axis=1)
    gate_bias = jnp.pad(jnp.concatenate([b_igate[l], b_fgate[l]]), (0, LANES - 2 * M_HEADS)).reshape(1, LANES)
    reps = D_WIDTH // D_HEAD_DIM
    smalls = [conv_w[l], conv_b[l].reshape(1, -1), gate_bias,
              jnp.tile(q_norm_w[l], reps).reshape(1, D_WIDTH), jnp.tile(k_norm_w[l], reps).reshape(1, D_WIDTH),
              _rope_freq_row(), _block_diag_ones()]
    qm, km, vm, so, gates, qd, kd, vd, sgm, sgd = _in_call(
        x2, pos2, mod3, norm1_w[l].reshape(1, d), w_packed, smalls, seq)

    hm = _mlstm_call(qm, km, vm, so, gates, mlstm_norm_w[l].reshape(1, M_WIDTH), bsz, seq)

    lam_rows = jnp.stack([lam_q1[l], lam_k1[l], lam_q2[l], lam_k2[l]])
    hd = _attn_call(qd, kd, vd, lam_rows, subln_w[l].reshape(1, D_V_DIM), jnp.stack([q_norm_w[l], k_norm_w[l]]),
                    bsz, seq)

    w_r = jnp.pad(jnp.concatenate([w_rg[l], w_re[l]], axis=1), ((0, 0), (0, LANES - N_GROUPS - N_EXPERTS)))
    wr_hi = w_r.astype(BF16)
    wr_lo = (w_r - wr_hi.astype(F32)).astype(BF16)
    b_r = jnp.pad(jnp.concatenate([b_rg[l], b_re[l]]), (0, LANES - N_GROUPS - N_EXPERTS)).reshape(1, LANES)
    x1, h2, dest, rw, cnt = _merge_call(x2, hm, hd, sgm, sgd, mod3, w_br_m[l].astype(BF16), w_br_d[l].astype(BF16),
                                        w_out[l].astype(BF16), norm2_w[l].reshape(1, d), wr_hi, wr_lo, b_r, seq)

    tiles_per_seq = seq // TM_MERGE
    meta = cnt[tiles_per_seq - 1::tiles_per_seq, N_GROUPS:N_GROUPS + N_EXPERTS, 0].astype(jnp.int32)
    meta = jnp.pad(meta, ((0, 0), (0, LANES - N_EXPERTS)))
    interleave = lambda a: a[:, 0:2, :].transpose(0, 2, 1).reshape(bsz, 2 * seq)
    dest = interleave(dest)
    rw = jnp.pad(interleave(rw), ((0, 0), (0, LANES)))
    wgu = jnp.concatenate([w_gate[l], w_up[l]], axis=-1).astype(BF16)
    tok_tiles = (bsz, seq * SUBLANES, LANES)
    return _moe_call(dest, rw, meta, h2.reshape(tok_tiles), x1.reshape(tok_tiles), mod3, wgu,
                     w_down[l].astype(BF16))
```

```python
import functools
import math

import numpy as np
import jax
import jax.numpy as jnp
from jax import lax
from jax.experimental import pallas as pl
from jax.experimental.pallas import tpu as pltpu

F32 = jnp.float32
BF16 = jnp.bfloat16

D_MODEL = 1024
M_HEADS = 4
M_HEAD_DIM = 128
M_WIDTH = M_HEADS * M_HEAD_DIM
CONV_WIDTH = 4
D_HEADS = 4
D_HEAD_DIM = 64
D_V_DIM = 2 * D_HEAD_DIM
D_WIDTH = D_HEADS * D_V_DIM
ROPE_THETA = 500000.0
ROPE_DIM = D_HEAD_DIM // 4
N_GROUPS = 4
EXPERTS_PER_GROUP = 8
N_EXPERTS = N_GROUPS * EXPERTS_PER_GROUP
D_EXPERT = D_MODEL // 4
EPS = 1e-6
LAM_INIT = 0.8 - 0.6 * math.exp(-0.3 * 0)

LANES = 128
SUBLANES = 8
MXU_DIM = 256
VMEM_LIMIT = 56 * 1024 * 1024

TM_IN = 512
IN_SPLIT = 2
MERGE_SPLIT = 1
ROUTE_ROWS = 40
TM_MERGE = 512
M_CHUNK = 256
M_GROUP = 4
A_BLOCK = 512
LOG2_E = math.log2(math.e)
A_UNROLL = 4
BOUND_SLACK = 1.01
MOE_R = 256
MOE_RS = MOE_R + SUBLANES
MOE_U = 16
MOE_OUT_ROWS = 512
IN_SEGMENTS_A = (("qk", 2 * M_WIDTH), ("v", M_WIDTH), ("o", M_WIDTH))
IN_SEGMENTS_B = (("qd", D_WIDTH), ("kd", D_WIDTH), ("vd", D_WIDTH), ("gm", D_MODEL), ("gd", D_MODEL))
NEG_INF = float("-inf")


def _dot(a, b):
    return jnp.dot(a, b, preferred_element_type=F32)


def _dot_nt(a, b):
    return lax.dot_general(a, b, (((1,), (1,)), ((), ())), preferred_element_type=F32)


def _dot_tn(a, b):
    return lax.dot_general(a, b, (((0,), (0,)), ((), ())), preferred_element_type=F32)


def _sigmoid(v):
    return 0.5 * jnp.tanh(0.5 * v) + 0.5


def _const_spec(shape):
    nd = len(shape)
    return pl.BlockSpec(shape, lambda *_: (0,) * nd)


def _ada_kernel(c_ref, w_ref, b_ref, o_ref):
    o_ref[...] = _dot(c_ref[...].astype(BF16), w_ref[...].astype(BF16)) + b_ref[...]


def _ada_call(c, w_ada, b_ada):
    bsz, d = c.shape
    cols = w_ada.shape[1]
    tn = 1024
    return pl.pallas_call(
        _ada_kernel,
        grid=(cols // tn,),
        in_specs=[pl.BlockSpec((bsz, d), lambda j: (0, 0)),
                  pl.BlockSpec((d, tn), lambda j: (0, j)),
                  pl.BlockSpec((1, tn), lambda j: (0, j))],
        out_specs=pl.BlockSpec((bsz, tn), lambda j: (0, j)),
        out_shape=jax.ShapeDtypeStruct((bsz, cols), F32),
        compiler_params=pltpu.CompilerParams(dimension_semantics=("arbitrary",)),
    )(c, w_ada, b_ada.reshape(1, cols))


def _in_kernel(tiles_per_seq,
               x_ref, pos_ref, mod_ref, n1w_ref,
               wa_ref, wg_ref, wb_ref, cw_ref, cb_ref, gb_ref, qnw_ref, knw_ref, freq_ref, bd_ref,
               qm_ref, km_ref, vm_ref, so_ref, gates_ref, qd_ref, kd_ref, vd_ref, sgm_ref, sgd_ref,
               cbuf):
    tm = x_ref.shape[0]
    i = pl.program_id(0)
    seg = {"g": wg_ref}
    for ref, segments in ((wa_ref, IN_SEGMENTS_A), (wb_ref, IN_SEGMENTS_B)):
        o = 0
        for name, width in segments:
            seg[name] = ref.at[:, o:o + width]
            o += width

    @pl.when(i == 0)
    def _():
        cbuf[...] = jnp.zeros(cbuf.shape, F32)

    prev = jnp.where(i % tiles_per_seq == 0, 0.0, cbuf[...])
    rows = tm // IN_SPLIT
    for part in range(IN_SPLIT):
        prev = _in_rows(slice(part * rows, (part + 1) * rows), prev,
                        x_ref, pos_ref, mod_ref, n1w_ref, seg, cw_ref, cb_ref, gb_ref, qnw_ref, knw_ref, freq_ref,
                        bd_ref, qm_ref, km_ref, vm_ref, so_ref, gates_ref, qd_ref, kd_ref, vd_ref, sgm_ref, sgd_ref)
    cbuf[...] = prev


def _in_rows(rs, prev, x_ref, pos_ref, mod_ref, n1w_ref, seg, cw_ref, cb_ref, gb_ref, qnw_ref, knw_ref, freq_ref,
             bd_ref, qm_ref, km_ref, vm_ref, so_ref, gates_ref, qd_ref, kd_ref, vd_ref, sgm_ref, sgd_ref):
    wqk_ref, wv_ref, wo_ref, wg_ref = seg["qk"], seg["v"], seg["o"], seg["g"]
    wqd_ref, wkd_ref, wvd_ref, wgm_ref, wgd_ref = seg["qd"], seg["kd"], seg["vd"], seg["gm"], seg["gd"]
    x = x_ref[rs, :]
    tm = x.shape[0]
    ms = jnp.mean(x * x, axis=-1, keepdims=True)
    scale1 = mod_ref[0, 1:2, :]
    shift1 = mod_ref[0, 0:1, :]
    h = (x * lax.rsqrt(ms + EPS) * n1w_ref[...]) * (1.0 + scale1) + shift1
    hb = h.astype(BF16)

    y = _dot(hb, wqk_ref[...])
    row8 = lax.broadcasted_iota(jnp.int32, (SUBLANES, 2 * M_WIDTH), 0)
    acc = cb_ref[...] + cw_ref[3:4, :] * y
    for d in range(1, CONV_WIDTH):
        shifted = pltpu.roll(y, d, 0)
        head = jnp.where(row8 < d, pltpu.roll(prev, d, 0), shifted[:SUBLANES])
        shifted = jnp.concatenate([head, shifted[SUBLANES:]], axis=0)
        acc = acc + cw_ref[3 - d:4 - d, :] * shifted
    a = acc * _sigmoid(acc)
    qm_ref[rs, :] = a[:, :M_WIDTH].astype(BF16)
    km_ref[rs, :] = (a[:, M_WIDTH:] * (M_HEAD_DIM ** -0.5)).astype(BF16)

    sgm_ref[rs, :] = _sigmoid(_dot(hb, wgm_ref[...])).astype(BF16)
    vm_ref[rs, :] = _dot(hb, wv_ref[...]).astype(BF16)
    so_ref[rs, :] = _sigmoid(_dot(hb, wo_ref[...])).astype(BF16)

    g = _dot(hb, wg_ref[...]) + gb_ref[...]
    lane = lax.broadcasted_iota(jnp.int32, g.shape, 1)
    logsig = jnp.minimum(g, 0.0) - jnp.log(1.0 + jnp.exp(-jnp.abs(g)))
    gates_ref[rs, :] = jnp.where(lane < M_HEADS, g, logsig)

    ang = pos_ref[rs, :].astype(F32) * freq_ref[...]
    cos = jnp.cos(ang)
    sin = jnp.sin(ang)
    lane_in_map = lax.broadcasted_iota(jnp.int32, ang.shape, 1) % D_HEAD_DIM
    half = ROPE_DIM // 2
    s_from_lo = jnp.where(lane_in_map >= half, sin, 0.0)
    s_from_hi = jnp.where(lane_in_map < half, -sin, 0.0)
    reps = D_WIDTH // LANES
    cos_w = jnp.concatenate([cos] * reps, axis=1)
    slo_w = jnp.concatenate([s_from_lo] * reps, axis=1)
    shi_w = jnp.concatenate([s_from_hi] * reps, axis=1)

    def qk_norm_rope(w_ref, nw_ref, out_scale):
        t = _dot(hb, w_ref[...])
        sq = (t * t).astype(BF16)
        gs = [_dot(sq[:, c * MXU_DIM:(c + 1) * MXU_DIM], bd_ref[...]) for c in range(D_WIDTH // MXU_DIM)]
        msq = jnp.concatenate(gs, axis=1) * (1.0 / D_HEAD_DIM)
        tn = t * lax.rsqrt(msq + EPS) * nw_ref[...]
        rot = (tn * cos_w + pltpu.roll(tn, half, 1) * slo_w
               + pltpu.roll(tn, D_WIDTH - half, 1) * shi_w)
        return (rot * out_scale).astype(BF16)

    qd_ref[rs, :] = qk_norm_rope(wqd_ref, qnw_ref, D_HEAD_DIM ** -0.5 * LOG2_E)
    sgd_ref[rs, :] = _sigmoid(_dot(hb, wgd_ref[...])).astype(BF16)
    kd_ref[rs, :] = qk_norm_rope(wkd_ref, knw_ref, 1.0)
    vd_ref[rs, :] = _dot(hb, wvd_ref[...]).astype(BF16)
    return y[tm - SUBLANES:, :]


def _in_call(x2, pos2, mod3, n1w, w_packed, smalls, seq):
    n, d = x2.shape
    tm = TM_IN
    tiles_per_seq = seq // tm
    row = lambda w: pl.BlockSpec((tm, w), lambda i: (i, 0))
    in_specs = [row(d), row(1),
                pl.BlockSpec((1, 6, d), lambda i: (i // tiles_per_seq, 0, 0)),
                _const_spec(n1w.shape)]
    in_specs += [_const_spec(s.shape) for s in (*w_packed, *smalls)]
    out_widths = [(M_WIDTH, BF16), (M_WIDTH, BF16), (M_WIDTH, BF16), (M_WIDTH, BF16), (LANES, F32),
                  (D_WIDTH, BF16), (D_WIDTH, BF16), (D_WIDTH, BF16), (d, BF16), (d, BF16)]
    return pl.pallas_call(
        functools.partial(_in_kernel, tiles_per_seq),
        grid=(n // tm,),
        in_specs=in_specs,
        out_specs=[row(w) for w, _ in out_widths],
        out_shape=[jax.ShapeDtypeStruct((n, w), dt) for w, dt in out_widths],
        scratch_shapes=[pltpu.VMEM((SUBLANES, 2 * M_WIDTH), F32)],
        compiler_params=pltpu.CompilerParams(dimension_semantics=("arbitrary",),
                                             vmem_limit_bytes=VMEM_LIMIT),
    )(x2, pos2, mod3, n1w, *w_packed, *smalls)


def _mlstm_kernel(grp, q_ref, k_ref, v_ref, so_ref, gates_ref, nw_ref, out_ref, cx_ref, m_ref):
    @pl.when(pl.program_id(1) == 0)
    def _():
        cx_ref[...] = jnp.zeros(cx_ref.shape, F32)
        m_ref[...] = jnp.zeros(m_ref.shape, F32)

    for gi in range(grp):
        _mlstm_chunk(q_ref.at[gi], k_ref.at[gi], v_ref.at[gi], so_ref.at[gi], gates_ref.at[gi], nw_ref,
                     out_ref.at[gi], cx_ref.at[gi], m_ref.at[gi])


def _mlstm_chunk(q_ref, k_ref, v_ref, so_ref, gates_ref, nw_ref, out_ref, cx_ref, m_ref):
    L = q_ref.shape[0]
    hd = M_HEAD_DIM
    g = gates_ref[...]
    r_i = lax.broadcasted_iota(jnp.int32, (L, L), 0)
    c_i = lax.broadcasted_iota(jnp.int32, (L, L), 1)
    causal = c_i <= r_i
    tri = causal.astype(BF16)
    g_hi = g.astype(BF16)
    g_mid = (g - g_hi.astype(F32)).astype(BF16)
    g_lo = (g - g_hi.astype(F32) - g_mid.astype(F32)).astype(BF16)
    bc = _dot(tri, g_hi) + _dot(tri, g_mid) + _dot(tri, g_lo)
    g_t = g.T
    bc_t = bc.T
    ones_col = jnp.where(lax.broadcasted_iota(jnp.int32, (L, hd), 1) == 0, 1.0, 0.0).astype(BF16)

    causal_t = r_i <= c_i
    for h in range(M_HEADS):
        sl = slice(h * hd, (h + 1) * hd)
        i_row = g_t[h:h + 1, :]
        b_row = bc_t[M_HEADS + h:M_HEADS + h + 1, :]
        r_col = g[:, h:h + 1] - bc[:, M_HEADS + h:M_HEADS + h + 1]
        m_prev = m_ref[h][0:1, 0:1]
        cxt = cx_ref[h]

        d_t = jnp.where(causal_t, b_row + r_col, NEG_INF)
        m_inter = b_row + m_prev
        m_row = jnp.maximum(m_inter, jnp.max(d_t, axis=0, keepdims=True))
        q = q_ref[:, sl]
        k = k_ref[:, sl]
        vx = jnp.concatenate([v_ref[:, sl], ones_col], axis=1)
        s_t = _dot_nt(k, q) * jnp.exp(d_t - m_row)
        inter = jnp.exp(m_inter - m_row)
        num_t = _dot_tn(vx, s_t.astype(BF16)) + inter * _dot_nt(cxt.astype(BF16), q)
        den = jnp.maximum(jnp.abs(num_t[hd:hd + 1, :]), jnp.exp(-m_row))
        h_t = num_t[:hd, :] * (1.0 / den)

        b_last = b_row[:, L - 1:L]
        m_new = jnp.maximum(b_last + m_prev, jnp.max(b_last + (i_row - b_row), axis=1, keepdims=True))
        wexp = jnp.exp(b_last + r_col - m_new)
        decay = jnp.exp(b_last + m_prev - m_new)
        kw = (k.astype(F32) * wexp).astype(BF16)
        cx_ref[h] = decay * cxt + _dot_tn(vx, kw)
        m_ref[h] = jnp.broadcast_to(m_new, m_ref.shape[1:])

        hn_t = h_t * lax.rsqrt(jnp.mean(h_t * h_t, axis=0, keepdims=True) + EPS)
        out_ref[:, sl] = (hn_t.T * nw_ref[:, sl] * so_ref[:, sl].astype(F32)).astype(BF16)


def _mlstm_call(qm, km, vm, so, gates, nw, bsz, seq):
    L = M_CHUNK
    grp = math.gcd(bsz, M_GROUP)
    blk = lambda w: pl.BlockSpec((grp, L, w), lambda b, c: (b, c, 0))
    seqs = lambda a: a.reshape(bsz, seq, a.shape[-1])
    out = pl.pallas_call(
        functools.partial(_mlstm_kernel, grp),
        grid=(bsz // grp, seq // L),
        in_specs=[blk(M_WIDTH), blk(M_WIDTH), blk(M_WIDTH), blk(M_WIDTH), blk(LANES),
                  pl.BlockSpec((1, M_WIDTH), lambda b, c: (0, 0))],
        out_specs=blk(M_WIDTH),
        out_shape=jax.ShapeDtypeStruct((bsz, seq, M_WIDTH), BF16),
        scratch_shapes=[pltpu.VMEM((grp, M_HEADS, 2 * M_HEAD_DIM, M_HEAD_DIM), F32),
                        pltpu.VMEM((grp, M_HEADS, SUBLANES, LANES), F32)],
        compiler_params=pltpu.CompilerParams(dimension_semantics=("arbitrary", "arbitrary"),
                                             vmem_limit_bytes=VMEM_LIMIT),
    )(seqs(qm), seqs(km), seqs(vm), seqs(so), seqs(gates), nw)
    return out.reshape(bsz * seq, M_WIDTH)


def _attn_kernel(ii_ref, jj_ref, q_ref, k_ref, v_ref, lam_ref, sw_ref, nw_ref, out_ref,
                 vt_ref, kp_ref, qa_ref, acc_ref, l_ref):
    seq = q_ref.shape[0]
    t = A_BLOCK
    nq = seq // t
    lp = lam_ref[...]
    lam = (jnp.exp(jnp.sum(lp[0:1] * lp[1:2], axis=-1, keepdims=True))
           - jnp.exp(jnp.sum(lp[2:3] * lp[3:4], axis=-1, keepdims=True)) + LAM_INIT)
    for jb in range(nq):
        vt_ref[jb] = v_ref[jb * t:(jb + 1) * t, :].T

    nw = jnp.max(jnp.abs(nw_ref[...]), axis=-1, keepdims=True)
    bound = (LOG2_E * D_HEAD_DIM ** 0.5 * BOUND_SLACK) * nw[0:1] * nw[1:2]

    aux = (D_HEAD_DIM, 0)
    k = k_ref[...]
    q = q_ref[...]
    lane_s = lax.broadcasted_iota(jnp.int32, (seq, LANES), 1)
    in_map_s = (lane_s < D_HEAD_DIM, lane_s >= D_HEAD_DIM)
    for mp in range(2):
        kp_ref[mp] = jnp.where(in_map_s[mp], k, jnp.where(lane_s == aux[mp], 1.0, 0.0).astype(BF16))
        qa_ref[mp] = jnp.where(in_map_s[mp], q, jnp.where(lane_s == aux[mp], -bound, 0.0).astype(BF16))
    l_ref[...] = jnp.zeros(l_ref.shape, F32)
    acc_ref[...] = jnp.zeros(acc_ref.shape, F32)

    k_i = lax.broadcasted_iota(jnp.int32, (t, t), 0)
    q_i = lax.broadcasted_iota(jnp.int32, (t, t), 1)
    diag_mask = k_i <= q_i

    def step(i, j, masked):
        koff = pl.multiple_of(j * t, t)
        qoff = pl.multiple_of(i * t, t)
        vt = vt_ref[j]
        for mp in range(2):
            s = _dot_nt(kp_ref[mp, pl.ds(koff, t), :], qa_ref[mp, pl.ds(qoff, t), :])
            if masked:
                s = jnp.where(diag_mask, s, NEG_INF)
            p = jnp.exp2(s)
            l_ref[mp * nq + i] += jnp.sum(p, axis=0, keepdims=True)
            acc_ref[mp * nq + i] += _dot(vt, p.astype(BF16))

    def run(n_steps, step_at):
        trips = n_steps // A_UNROLL

        def body(u, c):
            for r in range(A_UNROLL):
                step_at(u * A_UNROLL + r)
            return c

        if trips:
            lax.fori_loop(0, trips, body, 0)
        for p in range(trips * A_UNROLL, n_steps):
            step_at(p)

    run(nq * (nq - 1) // 2, lambda p: step(ii_ref[p], jj_ref[p], False))
    run(nq, lambda p: step(p, p, True))

    def finish(i, c):
        o_t = (acc_ref[i] * (1.0 / l_ref[i]) - lam * (acc_ref[nq + i] * (1.0 / l_ref[nq + i])))
        nrm = lax.rsqrt(jnp.mean(o_t * o_t, axis=0, keepdims=True) + EPS)
        on = (o_t * nrm).T * sw_ref[...] * (1.0 - LAM_INIT)
        out_ref[pl.ds(pl.multiple_of(i * t, t), t), :] = on.astype(BF16)
        return c

    lax.fori_loop(0, nq, finish, 0)


def _attn_call(qd, kd, vd, lam_rows, subw, qk_nw, bsz, seq):
    t = A_BLOCK
    nq = seq // t
    pairs = [(i, j) for i in range(nq) for j in range(i)]
    ii = jnp.asarray([p[0] for p in pairs] or [0], jnp.int32)
    jj = jnp.asarray([p[1] for p in pairs] or [0], jnp.int32)
    blk = pl.BlockSpec((seq, LANES), lambda b, h, *_: (b, h))
    const = lambda shape: pl.BlockSpec(shape, lambda b, h, *_: (0,) * len(shape))
    return pl.pallas_call(
        _attn_kernel,
        grid_spec=pltpu.PrefetchScalarGridSpec(
            num_scalar_prefetch=2,
            grid=(bsz, D_HEADS),
            in_specs=[blk, blk, blk, const(lam_rows.shape), const(subw.shape), const(qk_nw.shape)],
            out_specs=blk,
            scratch_shapes=[pltpu.VMEM((nq, D_V_DIM, t), BF16),
                            pltpu.VMEM((2, seq, LANES), BF16),
                            pltpu.VMEM((2, seq, LANES), BF16),
                            pltpu.VMEM((2 * nq, D_V_DIM, t), F32),
                            pltpu.VMEM((2 * nq, 1, t), F32)]),
        out_shape=jax.ShapeDtypeStruct((bsz * seq, D_WIDTH), BF16),
        compiler_params=pltpu.CompilerParams(dimension_semantics=("arbitrary", "arbitrary"),
                                             vmem_limit_bytes=VMEM_LIMIT),
    )(ii, jj, qd, kd, vd, lam_rows, subw, qk_nw)


def _merge_kernel(tiles_per_seq, x_ref, hm_ref, hd_ref, sgm_ref, sgd_ref, mod_ref, wbm_ref, wbd_ref, wout_ref,
                  n2w_ref, wr_hi_ref, wr_lo_ref, br_ref, x1_ref, h2_ref, dest_ref, rw_ref, cnt_ref, carry_ref):
    gate1 = mod_ref[0, 2:3, :]
    shift2 = mod_ref[0, 3:4, :]
    scale2 = mod_ref[0, 4:5, :]
    tm = x_ref.shape[0]
    rows = tm // MERGE_SPLIT
    parts = []
    for part in range(MERGE_SPLIT):
        rs = slice(part * rows, (part + 1) * rows)
        merged = (sgm_ref[rs, :].astype(F32) * _dot(hm_ref[rs, :], wbm_ref[...])
                  + sgd_ref[rs, :].astype(F32) * _dot(hd_ref[rs, :], wbd_ref[...]))
        x1 = x_ref[rs, :] + gate1 * _dot(merged.astype(BF16), wout_ref[...])
        ms = jnp.mean(x1 * x1, axis=-1, keepdims=True)
        h2 = (x1 * lax.rsqrt(ms + EPS) * n2w_ref[...]) * (1.0 + scale2) + shift2
        for ch in range(D_MODEL // LANES):
            cols = slice(ch * LANES, (ch + 1) * LANES)
            x1_ref[pl.ds(part * rows * SUBLANES + ch, rows, stride=SUBLANES), :] = x1[:, cols]
            h2_ref[pl.ds(part * rows * SUBLANES + ch, rows, stride=SUBLANES), :] = h2[:, cols]
        h2_hi = h2.astype(BF16)
        h2_lo = (h2 - h2_hi.astype(F32)).astype(BF16)
        parts.append(_dot(h2_hi, wr_hi_ref[...]) + _dot(h2_lo, wr_hi_ref[...]) + _dot(h2_hi, wr_lo_ref[...]))
    logits = (jnp.concatenate(parts, axis=0) + br_ref[...]).T[0:ROUTE_ROWS, :]
    row = lax.broadcasted_iota(jnp.int32, logits.shape, 0)
    row_f = row.astype(F32)
    big = float(LANES)

    def top(vals):
        mx = jnp.max(vals, axis=0, keepdims=True)
        idx = jnp.min(jnp.where(vals == mx, row_f, big), axis=0, keepdims=True)
        return mx, idx

    is_group = row < N_GROUPS
    gmax, gidx = top(jnp.where(is_group, logits, NEG_INF))
    gsum = jnp.sum(jnp.where(is_group, jnp.exp(logits - gmax), 0.0), axis=0, keepdims=True)
    pg_top = 1.0 / gsum
    lo = N_GROUPS + EXPERTS_PER_GROUP * gidx
    in_group = (row_f >= lo) & (row_f < lo + EXPERTS_PER_GROUP)
    el = jnp.where(in_group, logits, NEG_INF)
    e1max, e1idx = top(el)
    e2max, e2idx = top(jnp.where(row_f == e1idx, NEG_INF, el))
    p2 = jnp.exp(e2max - e1max)
    w1 = pg_top / (1.0 + p2)
    w2 = w1 * p2

    @pl.when(pl.program_id(0) % tiles_per_seq == 0)
    def _():
        carry_ref[...] = jnp.zeros(carry_ref.shape, F32)

    is_e1 = row_f == e1idx
    is_e2 = row_f == e2idx
    onehot = jnp.where(is_e1 | is_e2, 1.0, 0.0)
    r_i = lax.broadcasted_iota(jnp.int32, (tm, tm), 0)
    c_i = lax.broadcasted_iota(jnp.int32, (tm, tm), 1)
    before = jnp.where(r_i < c_i, 1.0, 0.0).astype(BF16)
    cum = _dot(onehot.astype(BF16), before) + carry_ref[...]
    r1 = jnp.sum(jnp.where(is_e1, cum, 0.0), axis=0, keepdims=True)
    r2 = jnp.sum(jnp.where(is_e2, cum, 0.0), axis=0, keepdims=True)
    carry = carry_ref[...] + jnp.sum(onehot, axis=1, keepdims=True)
    carry_ref[...] = carry
    cnt_ref[0] = carry

    seq_len = float(tiles_per_seq * tm)
    row8 = lax.broadcasted_iota(jnp.int32, (SUBLANES, tm), 0)
    pos = jnp.where(row8 == 0, (e1idx - N_GROUPS) * seq_len + r1, (e2idx - N_GROUPS) * seq_len + r2)
    dest_ref[0] = pos.astype(jnp.int32)
    rw_ref[0] = jnp.where(row8 == 0, w1, w2)


def _merge_call(x2, hm, hd, sgm, sgd, mod3, wbm, wbd, wout, n2w, wr_hi, wr_lo, br, seq):
    n, d = x2.shape
    tm = TM_MERGE
    tiles_per_seq = seq // tm
    row = lambda w: pl.BlockSpec((tm, w), lambda i: (i, 0))
    consts = [wbm, wbd, wout, n2w, wr_hi, wr_lo, br]
    per_seq = pl.BlockSpec((1, SUBLANES, tm), lambda i: (i // tiles_per_seq, 0, i % tiles_per_seq))
    return pl.pallas_call(
        functools.partial(_merge_kernel, tiles_per_seq),
        grid=(n // tm,),
        in_specs=[row(d), row(M_WIDTH), row(D_WIDTH), row(d), row(d),
                  pl.BlockSpec((1, 6, d), lambda i: (i // tiles_per_seq, 0, 0))]
                 + [_const_spec(a.shape) for a in consts],
        out_specs=[pl.BlockSpec((tm * SUBLANES, LANES), lambda i: (i, 0)),
                   pl.BlockSpec((tm * SUBLANES, LANES), lambda i: (i, 0)),
                   per_seq, per_seq, pl.BlockSpec((1, ROUTE_ROWS, 1), lambda i: (i, 0, 0))],
        out_shape=[jax.ShapeDtypeStruct((n * SUBLANES, LANES), F32), jax.ShapeDtypeStruct((n * SUBLANES, LANES), F32),
                   jax.ShapeDtypeStruct((n // seq, SUBLANES, seq), jnp.int32),
                   jax.ShapeDtypeStruct((n // seq, SUBLANES, seq), F32),
                   jax.ShapeDtypeStruct((n // tm, ROUTE_ROWS, 1), F32)],
        scratch_shapes=[pltpu.VMEM((ROUTE_ROWS, 1), F32)],
        compiler_params=pltpu.CompilerParams(dimension_semantics=("arbitrary",),
                                             vmem_limit_bytes=VMEM_LIMIT),
    )(x2, hm, hd, sgm, sgd, mod3, *consts)


def _moe_kernel(dest_hbm, rw_hbm, meta_hbm, h2_hbm, x1_hbm, mod_ref, wg_hbm, wu_hbm, wd_hbm, out_hbm,
                dest, rw, meta, inv, h2buf, acc, wg_buf, wu_buf, wd_buf, xst, yst, ost, sems):
    b = pl.program_id(0)
    t_blk = h2buf.shape[0] // SUBLANES - 1
    spare = 2 * t_blk
    r = MOE_R
    tile = lambda ref, tok: ref.at[pl.ds(pl.multiple_of(tok * SUBLANES, SUBLANES), SUBLANES), :]

    def weight_copies(e, slot):
        return (pltpu.make_async_copy(wg_hbm.at[e], wg_buf.at[slot], sems.at[5 + slot]),
                pltpu.make_async_copy(wu_hbm.at[e], wu_buf.at[slot], sems.at[7 + slot]),
                pltpu.make_async_copy(wd_hbm.at[e], wd_buf.at[slot], sems.at[9 + slot]))

    route_copies = (pltpu.make_async_copy(dest_hbm.at[b], dest, sems.at[0]),
                    pltpu.make_async_copy(rw_hbm.at[b], rw, sems.at[1]),
                    pltpu.make_async_copy(meta_hbm.at[b], meta, sems.at[2]))
    h2_copy = pltpu.make_async_copy(h2_hbm.at[b], h2buf.at[pl.ds(0, t_blk * SUBLANES)], sems.at[3])
    x1_copy = pltpu.make_async_copy(x1_hbm.at[b], acc.at[pl.ds(0, t_blk * SUBLANES)], sems.at[4])

    for cp in route_copies + (h2_copy, x1_copy) + weight_copies(0, 0):
        cp.start()
    xst[...] = jnp.zeros(xst.shape, F32)
    yst[...] = jnp.zeros(yst.shape, F32)
    tile(h2buf, t_blk)[...] = jnp.zeros((SUBLANES, LANES), F32)
    tile(acc, t_blk)[...] = jnp.zeros((SUBLANES, LANES), F32)
    for cp in route_copies:
        cp.wait()

    last = inv.shape[0] - 1

    def pad_body(e, c):
        for j in range(MOE_U - 1):
            inv[jnp.minimum(e * t_blk + meta[e] + j, last)] = spare
        return c

    lax.fori_loop(0, N_EXPERTS, pad_body, 0)

    def inv_body(u, c):
        for j in range(MOE_U):
            s = u * MOE_U + j
            inv[dest[s]] = s
        return c

    lax.fori_loop(0, spare // MOE_U, inv_body, 0)
    h2_copy.wait()
    x1_copy.wait()
    gate2 = mod_ref[0, 5:6, :]
    n_chunks = D_MODEL // LANES

    def expert_body(e, c):
        slot = e % 2
        for cp in weight_copies(e, slot):
            cp.wait()

        @pl.when(e + 1 < N_EXPERTS)
        def _():
            for cp in weight_copies(e + 1, 1 - slot):
                cp.start()

        seg = e * t_blk
        n = meta[e]

        def run_tile(base, count, rows):
            groups = (count + MOE_U - 1) // MOE_U

            def gather(u, c3):
                row0 = u * MOE_U
                pos0 = base + row0
                for j in range(MOE_U):
                    xst[pl.ds(row0 + j, SUBLANES, stride=MOE_RS), :] = tile(h2buf, inv[pos0 + j] >> 1)[...]
                return c3

            lax.fori_loop(0, groups, gather, 0)
            x = jnp.concatenate([xst[ch * MOE_RS:ch * MOE_RS + rows, :] for ch in range(n_chunks)], axis=1)
            xb = x.astype(BF16)
            g = _dot(xb, wg_buf[slot])
            act = (g * _sigmoid(g) * _dot(xb, wu_buf[slot])).astype(BF16)
            y = _dot(act, wd_buf[slot]) * gate2
            for ch in range(n_chunks):
                yst[ch * MOE_RS:ch * MOE_RS + rows, :] = y[:, ch * LANES:(ch + 1) * LANES]

            def scatter(u, c3):
                toks, vals = [], []
                row0 = u * MOE_U
                pos0 = base + row0
                for j in range(MOE_U):
                    v = inv[pos0 + j]
                    toks.append(v >> 1)
                    vals.append(tile(acc, v >> 1)[...]
                                + rw[v] * yst[pl.ds(row0 + j, SUBLANES, stride=MOE_RS), :])
                for tok, val in zip(toks, vals):
                    tile(acc, tok)[...] = val
                return c3

            lax.fori_loop(0, groups, scatter, 0)

        def full_tile(k, c2):
            run_tile(seg + k * r, r, r)
            return c2

        n_full = n // r
        lax.fori_loop(0, n_full, full_tile, 0)
        rem = n - n_full * r

        @pl.when(rem > r // 2)
        def _():
            run_tile(seg + n_full * r, rem, r)

        @pl.when((rem > 0) & (rem <= r // 2))
        def _():
            run_tile(seg + n_full * r, rem, r // 2)

        return c

    lax.fori_loop(0, N_EXPERTS, expert_body, 0)

    rows = ost.shape[1]
    n_pieces = t_blk // rows

    def out_copy(p):
        return pltpu.make_async_copy(ost.at[p % 2], out_hbm.at[b, pl.ds(p * rows, rows)], sems.at[11 + p % 2])

    for p in range(n_pieces):
        if p >= 2:
            out_copy(p - 2).wait()
        for ch in range(n_chunks):
            ost[p % 2, :, ch * LANES:(ch + 1) * LANES] = acc[pl.ds(p * rows * SUBLANES + ch, rows, stride=SUBLANES), :]
        out_copy(p).start()
    for p in range(max(0, n_pieces - 2), n_pieces):
        out_copy(p).wait()


def _moe_call(dest, rw, meta, h2, x1, mod3, wg, wu, wd):
    nb = h2.shape[0]
    t_blk = h2.shape[1] // SUBLANES
    d = D_MODEL
    tok_rows = (t_blk + 1) * SUBLANES
    any_spec = pl.BlockSpec(memory_space=pl.ANY)
    return pl.pallas_call(
        _moe_kernel,
        grid=(nb,),
        in_specs=[any_spec, any_spec, any_spec, any_spec, any_spec,
                  pl.BlockSpec((1, 6, d), lambda i: (i, 0, 0)), any_spec, any_spec, any_spec],
        out_specs=any_spec,
        out_shape=jax.ShapeDtypeStruct((nb, t_blk, d), F32),
        scratch_shapes=[pltpu.SMEM(dest.shape[1:], jnp.int32), pltpu.SMEM(rw.shape[1:], F32),
                        pltpu.SMEM(meta.shape[1:], jnp.int32), pltpu.SMEM((N_EXPERTS * t_blk,), jnp.int32),
                        pltpu.VMEM((tok_rows, LANES), F32), pltpu.VMEM((tok_rows, LANES), F32),
                        pltpu.VMEM((2, d, D_EXPERT), BF16), pltpu.VMEM((2, d, D_EXPERT), BF16),
                        pltpu.VMEM((2, D_EXPERT, d), BF16),
                        pltpu.VMEM((SUBLANES * MOE_RS, LANES), F32), pltpu.VMEM((SUBLANES * MOE_RS, LANES), F32),
                        pltpu.VMEM((2, min(MOE_OUT_ROWS, t_blk), d), F32),
                        pltpu.SemaphoreType.DMA((13,))],
        compiler_params=pltpu.CompilerParams(dimension_semantics=("arbitrary",),
                                             vmem_limit_bytes=VMEM_LIMIT),
    )(dest, rw, meta, h2, x1, mod3, wg, wu, wd)


def _rope_freq_row():
    inv = ROPE_THETA ** (-np.arange(0, ROPE_DIM, 2, dtype=np.float32) / ROPE_DIM)
    row = np.zeros((LANES,), np.float32)
    half = ROPE_DIM // 2
    for base in range(0, LANES, D_HEAD_DIM):
        row[base:base + half] = inv
        row[base + half:base + ROPE_DIM] = inv
    return jnp.asarray(row.reshape(1, LANES))


def _block_diag_ones():
    idx = np.arange(MXU_DIM) // D_HEAD_DIM
    return jnp.asarray((idx[:, None] == idx[None, :]).astype(np.float32), dtype=BF16)


def kernel(x, c, positions, w_ada, b_ada, norm1_w, w_in, b_igate, b_fgate, conv_w, conv_b, mlstm_norm_w, q_norm_w, k_norm_w, lam_q1, lam_k1, lam_q2, lam_k2, subln_w, w_br_m, w_br_d, w_out, norm2_w, w_rg, b_rg, w_re, b_re, w_gate, w_up, w_down):
    bsz, seq, d = x.shape
    n = bsz * seq
    x2 = x.reshape(n, d)
    pos2 = positions.reshape(n, 1)
    l = 0

    mod3 = _ada_call(c, w_ada[l], b_ada[l]).reshape(bsz, 6, d)

    wi = w_in[l]
    gate_lo = 4 * M_WIDTH
    gate_hi = gate_lo + 2 * M_HEADS
    w_packed = (wi[:, :gate_lo].astype(BF16),
                jnp.pad(wi[:, gate_lo:gate_hi], ((0, 0), (0, LANES - 2 * M_HEADS))).astype(BF16),
                wi[:, gate_hi:].astype(BF16))
    gate_bias = jnp.pad(jnp.concatenate([b_igate[l], b_fgate[l]]), (0, LANES - 2 * M_HEADS)).reshape(1, LANES)
    reps = D_WIDTH // D_HEAD_DIM
    smalls = [conv_w[l], conv_b[l].reshape(1, -1), gate_bias,
              jnp.tile(q_norm_w[l], reps).reshape(1, D_WIDTH), jnp.tile(k_norm_w[l], reps).reshape(1, D_WIDTH),
              _rope_freq_row(), _block_diag_ones()]
    qm, km, vm, so, gates, qd, kd, vd, sgm, sgd = _in_call(
        x2, pos2, mod3, norm1_w[l].reshape(1, d), w_packed, smalls, seq)

    hm = _mlstm_call(qm, km, vm, so, gates, mlstm_norm_w[l].reshape(1, M_WIDTH), bsz, seq)

    lam_rows = jnp.stack([lam_q1[l], lam_k1[l], lam_q2[l], lam_k2[l]])
    hd = _attn_call(qd, kd, vd, lam_rows, subln_w[l].reshape(1, D_V_DIM), jnp.stack([q_norm_w[l], k_norm_w[l]]),
                    bsz, seq)

    w_r = jnp.pad(jnp.concatenate([w_rg[l], w_re[l]], axis=1), ((0, 0), (0, LANES - N_GROUPS - N_EXPERTS)))
    wr_hi = w_r.astype(BF16)
    wr_lo = (w_r - wr_hi.astype(F32)).astype(BF16)
    b_r = jnp.pad(jnp.concatenate([b_rg[l], b_re[l]]), (0, LANES - N_GROUPS - N_EXPERTS)).reshape(1, LANES)
    x1, h2, dest, rw, cnt = _merge_call(x2, hm, hd, sgm, sgd, mod3, w_br_m[l].astype(BF16), w_br_d[l].astype(BF16),
                                        w_out[l].astype(BF16), norm2_w[l].reshape(1, d), wr_hi, wr_lo, b_r, seq)

    tiles_per_seq = seq // TM_MERGE
    meta = cnt[tiles_per_seq - 1::tiles_per_seq, N_GROUPS:N_GROUPS + N_EXPERTS, 0].astype(jnp.int32)
    meta = jnp.pad(meta, ((0, 0), (0, LANES - N_EXPERTS)))
    interleave = lambda a: a[:, 0:2, :].transpose(0, 2, 1).reshape(bsz, 2 * seq)
    dest = interleave(dest)
    rw = jnp.pad(interleave(rw), ((0, 0), (0, LANES)))
    tok_tiles = (bsz, seq * SUBLANES, LANES)
    return _moe_call(dest, rw, meta, h2.reshape(tok_tiles), x1.reshape(tok_tiles), mod3,
                     w_gate[l].astype(BF16), w_up[l].astype(BF16), w_down[l].astype(BF16))
```

```python
import functools
import math

import numpy as np
import jax
import jax.numpy as jnp
from jax import lax
from jax.experimental import pallas as pl
from jax.experimental.pallas import tpu as pltpu

F32 = jnp.float32
BF16 = jnp.bfloat16

D_MODEL = 1024
M_HEADS = 4
M_HEAD_DIM = 128
M_WIDTH = M_HEADS * M_HEAD_DIM
CONV_WIDTH = 4
D_HEADS = 4
D_HEAD_DIM = 64
D_V_DIM = 2 * D_HEAD_DIM
D_WIDTH = D_HEADS * D_V_DIM
ROPE_THETA = 500000.0
ROPE_DIM = D_HEAD_DIM // 4
N_GROUPS = 4
EXPERTS_PER_GROUP = 8
N_EXPERTS = N_GROUPS * EXPERTS_PER_GROUP
D_EXPERT = D_MODEL // 4
EPS = 1e-6
LAM_INIT = 0.8 - 0.6 * math.exp(-0.3 * 0)

LANES = 128
SUBLANES = 8
MXU_DIM = 256
VMEM_LIMIT = 56 * 1024 * 1024

TM_IN = 512
IN_SPLIT = 1
MERGE_SPLIT = 1
ROUTE_ROWS = 40
TM_MERGE = 512
M_CHUNK = 256
M_GROUP = 4
A_BLOCK = 512
LOG2_E = math.log2(math.e)
A_UNROLL = 7
BOUND_SLACK = 1.01
MAX_FIXED_SHIFT = 60.0
MOE_R = 256
MOE_RS = MOE_R + SUBLANES
MOE_U = 16
MOE_OUT_ROWS = 512
IN_SEGMENTS = (("qk", 2 * M_WIDTH), ("v", M_WIDTH), ("o", M_WIDTH), ("g", LANES),
               ("qd", D_WIDTH), ("kd", D_WIDTH), ("vd", D_WIDTH), ("gm", D_MODEL), ("gd", D_MODEL))
NEG_INF = float("-inf")


def _dot(a, b):
    return jnp.dot(a, b, preferred_element_type=F32)


def _dot_nt(a, b):
    return lax.dot_general(a, b, (((1,), (1,)), ((), ())), preferred_element_type=F32)


def _dot_tn(a, b):
    return lax.dot_general(a, b, (((0,), (0,)), ((), ())), preferred_element_type=F32)


def _sigmoid(v):
    return 0.5 * jnp.tanh(0.5 * v) + 0.5


def _const_spec(shape):
    nd = len(shape)
    return pl.BlockSpec(shape, lambda *_: (0,) * nd)


def _ada_kernel(c_ref, w_ref, b_ref, o_ref):
    o_ref[...] = _dot(c_ref[...].astype(BF16), w_ref[...].astype(BF16)) + b_ref[...]


def _ada_call(c, w_ada, b_ada):
    bsz, d = c.shape
    cols = w_ada.shape[1]
    tn = 1024
    return pl.pallas_call(
        _ada_kernel,
        grid=(cols // tn,),
        in_specs=[pl.BlockSpec((bsz, d), lambda j: (0, 0)),
                  pl.BlockSpec((d, tn), lambda j: (0, j)),
                  pl.BlockSpec((1, tn), lambda j: (0, j))],
        out_specs=pl.BlockSpec((bsz, tn), lambda j: (0, j)),
        out_shape=jax.ShapeDtypeStruct((bsz, cols), F32),
        compiler_params=pltpu.CompilerParams(dimension_semantics=("arbitrary",)),
    )(c, w_ada, b_ada.reshape(1, cols))


def _in_kernel(tiles_per_seq,
               x_ref, pos_ref, mod_ref, n1w_ref,
               w_ref, cw_ref, cb_ref, gb_ref, qnw_ref, knw_ref, freq_ref, bd_ref,
               qm_ref, km_ref, vm_ref, so_ref, gates_ref, qd_ref, kd_ref, vd_ref, sgm_ref, sgd_ref,
               cbuf):
    tm = x_ref.shape[0]
    i = pl.program_id(0)
    seg, o = {}, 0
    for name, width in IN_SEGMENTS:
        seg[name] = w_ref.at[:, o:o + width]
        o += width

    @pl.when(i == 0)
    def _():
        cbuf[...] = jnp.zeros(cbuf.shape, F32)

    prev = jnp.where(i % tiles_per_seq == 0, 0.0, cbuf[...])
    rows = tm // IN_SPLIT
    for part in range(IN_SPLIT):
        prev = _in_rows(slice(part * rows, (part + 1) * rows), prev,
                        x_ref, pos_ref, mod_ref, n1w_ref, seg, cw_ref, cb_ref, gb_ref, qnw_ref, knw_ref, freq_ref,
                        bd_ref, qm_ref, km_ref, vm_ref, so_ref, gates_ref, qd_ref, kd_ref, vd_ref, sgm_ref, sgd_ref)
    cbuf[...] = prev


def _in_rows(rs, prev, x_ref, pos_ref, mod_ref, n1w_ref, seg, cw_ref, cb_ref, gb_ref, qnw_ref, knw_ref, freq_ref,
             bd_ref, qm_ref, km_ref, vm_ref, so_ref, gates_ref, qd_ref, kd_ref, vd_ref, sgm_ref, sgd_ref):
    wqk_ref, wv_ref, wo_ref, wg_ref = seg["qk"], seg["v"], seg["o"], seg["g"]
    wqd_ref, wkd_ref, wvd_ref, wgm_ref, wgd_ref = seg["qd"], seg["kd"], seg["vd"], seg["gm"], seg["gd"]
    x = x_ref[rs, :]
    tm = x.shape[0]
    ms = jnp.mean(x * x, axis=-1, keepdims=True)
    scale1 = mod_ref[0, 1:2, :]
    shift1 = mod_ref[0, 0:1, :]
    h = (x * lax.rsqrt(ms + EPS) * n1w_ref[...]) * (1.0 + scale1) + shift1
    hb = h.astype(BF16)

    y = _dot(hb, wqk_ref[...])
    row8 = lax.broadcasted_iota(jnp.int32, (SUBLANES, 2 * M_WIDTH), 0)
    acc = cb_ref[...] + cw_ref[3:4, :] * y
    for d in range(1, CONV_WIDTH):
        shifted = pltpu.roll(y, d, 0)
        head = jnp.where(row8 < d, pltpu.roll(prev, d, 0), shifted[:SUBLANES])
        shifted = jnp.concatenate([head, shifted[SUBLANES:]], axis=0)
        acc = acc + cw_ref[3 - d:4 - d, :] * shifted
    a = acc * _sigmoid(acc)
    qm_ref[rs, :] = a[:, :M_WIDTH].astype(BF16)
    km_ref[rs, :] = (a[:, M_WIDTH:] * (M_HEAD_DIM ** -0.5)).astype(BF16)

    sgm_ref[rs, :] = _sigmoid(_dot(hb, wgm_ref[...])).astype(BF16)
    vm_ref[rs, :] = _dot(hb, wv_ref[...]).astype(BF16)
    so_ref[rs, :] = _sigmoid(_dot(hb, wo_ref[...])).astype(BF16)

    g = _dot(hb, wg_ref[...]) + gb_ref[...]
    lane = lax.broadcasted_iota(jnp.int32, g.shape, 1)
    logsig = jnp.minimum(g, 0.0) - jnp.log(1.0 + jnp.exp(-jnp.abs(g)))
    gates_ref[rs, :] = jnp.where(lane < M_HEADS, g, logsig)

    ang = pos_ref[rs, :].astype(F32) * freq_ref[...]
    cos = jnp.cos(ang)
    sin = jnp.sin(ang)
    lane_in_map = lax.broadcasted_iota(jnp.int32, ang.shape, 1) % D_HEAD_DIM
    half = ROPE_DIM // 2
    s_from_lo = jnp.where(lane_in_map >= half, sin, 0.0)
    s_from_hi = jnp.where(lane_in_map < half, -sin, 0.0)
    reps = D_WIDTH // LANES
    cos_w = jnp.concatenate([cos] * reps, axis=1)
    slo_w = jnp.concatenate([s_from_lo] * reps, axis=1)
    shi_w = jnp.concatenate([s_from_hi] * reps, axis=1)

    def qk_norm_rope(w_ref, nw_ref, out_scale):
        t = _dot(hb, w_ref[...])
        sq = (t * t).astype(BF16)
        gs = [_dot(sq[:, c * MXU_DIM:(c + 1) * MXU_DIM], bd_ref[...]) for c in range(D_WIDTH // MXU_DIM)]
        msq = jnp.concatenate(gs, axis=1) * (1.0 / D_HEAD_DIM)
        tn = t * lax.rsqrt(msq + EPS) * nw_ref[...]
        rot = (tn * cos_w + pltpu.roll(tn, half, 1) * slo_w
               + pltpu.roll(tn, D_WIDTH - half, 1) * shi_w)
        return (rot * out_scale).astype(BF16)

    qd_ref[rs, :] = qk_norm_rope(wqd_ref, qnw_ref, D_HEAD_DIM ** -0.5 * LOG2_E)
    sgd_ref[rs, :] = _sigmoid(_dot(hb, wgd_ref[...])).astype(BF16)
    kd_ref[rs, :] = qk_norm_rope(wkd_ref, knw_ref, 1.0)
    vd_ref[rs, :] = _dot(hb, wvd_ref[...]).astype(BF16)
    return y[tm - SUBLANES:, :]


def _in_call(x2, pos2, mod3, n1w, w_packed, smalls, seq):
    n, d = x2.shape
    tm = TM_IN
    tiles_per_seq = seq // tm
    row = lambda w: pl.BlockSpec((tm, w), lambda i: (i, 0))
    in_specs = [row(d), row(1),
                pl.BlockSpec((1, 6, d), lambda i: (i // tiles_per_seq, 0, 0)),
                _const_spec(n1w.shape)]
    in_specs += [_const_spec(s.shape) for s in (w_packed, *smalls)]
    out_widths = [(M_WIDTH, BF16), (M_WIDTH, BF16), (M_WIDTH, BF16), (M_WIDTH, BF16), (LANES, F32),
                  (D_WIDTH, BF16), (D_WIDTH, BF16), (D_WIDTH, BF16), (d, BF16), (d, BF16)]
    return pl.pallas_call(
        functools.partial(_in_kernel, tiles_per_seq),
        grid=(n // tm,),
        in_specs=in_specs,
        out_specs=[row(w) for w, _ in out_widths],
        out_shape=[jax.ShapeDtypeStruct((n, w), dt) for w, dt in out_widths],
        scratch_shapes=[pltpu.VMEM((SUBLANES, 2 * M_WIDTH), F32)],
        compiler_params=pltpu.CompilerParams(dimension_semantics=("arbitrary",),
                                             vmem_limit_bytes=VMEM_LIMIT),
    )(x2, pos2, mod3, n1w, w_packed, *smalls)


def _mlstm_kernel(grp, q_ref, k_ref, v_ref, so_ref, gates_ref, nw_ref, out_ref, cx_ref, m_ref):
    @pl.when(pl.program_id(1) == 0)
    def _():
        cx_ref[...] = jnp.zeros(cx_ref.shape, F32)
        m_ref[...] = jnp.zeros(m_ref.shape, F32)

    for gi in range(grp):
        _mlstm_chunk(q_ref.at[gi], k_ref.at[gi], v_ref.at[gi], so_ref.at[gi], gates_ref.at[gi], nw_ref,
                     out_ref.at[gi], cx_ref.at[gi], m_ref.at[gi])


def _mlstm_chunk(q_ref, k_ref, v_ref, so_ref, gates_ref, nw_ref, out_ref, cx_ref, m_ref):
    L = q_ref.shape[0]
    hd = M_HEAD_DIM
    g = gates_ref[...]
    r_i = lax.broadcasted_iota(jnp.int32, (L, L), 0)
    c_i = lax.broadcasted_iota(jnp.int32, (L, L), 1)
    causal = c_i <= r_i
    tri = causal.astype(BF16)
    g_hi = g.astype(BF16)
    g_mid = (g - g_hi.astype(F32)).astype(BF16)
    g_lo = (g - g_hi.astype(F32) - g_mid.astype(F32)).astype(BF16)
    bc = _dot(tri, g_hi) + _dot(tri, g_mid) + _dot(tri, g_lo)
    g_t = g.T
    bc_t = bc.T
    ones_col = jnp.where(lax.broadcasted_iota(jnp.int32, (L, hd), 1) == 0, 1.0, 0.0).astype(BF16)

    causal_t = r_i <= c_i
    for h in range(M_HEADS):
        sl = slice(h * hd, (h + 1) * hd)
        i_row = g_t[h:h + 1, :]
        b_row = bc_t[M_HEADS + h:M_HEADS + h + 1, :]
        r_col = g[:, h:h + 1] - bc[:, M_HEADS + h:M_HEADS + h + 1]
        m_prev = m_ref[h][0:1, 0:1]
        cxt = cx_ref[h]

        d_t = jnp.where(causal_t, b_row + r_col, NEG_INF)
        m_inter = b_row + m_prev
        m_row = jnp.maximum(m_inter, jnp.max(d_t, axis=0, keepdims=True))
        q = q_ref[:, sl]
        k = k_ref[:, sl]
        vx = jnp.concatenate([v_ref[:, sl], ones_col], axis=1)
        s_t = _dot_nt(k, q) * jnp.exp(d_t - m_row)
        inter = jnp.exp(m_inter - m_row)
        num_t = _dot_tn(vx, s_t.astype(BF16)) + inter * _dot_nt(cxt.astype(BF16), q)
        den = jnp.maximum(jnp.abs(num_t[hd:hd + 1, :]), jnp.exp(-m_row))
        h_t = num_t[:hd, :] * (1.0 / den)

        b_last = b_row[:, L - 1:L]
        m_new = jnp.maximum(b_last + m_prev, jnp.max(b_last + (i_row - b_row), axis=1, keepdims=True))
        wexp = jnp.exp(b_last + r_col - m_new)
        decay = jnp.exp(b_last + m_prev - m_new)
        kw = (k.astype(F32) * wexp).astype(BF16)
        cx_ref[h] = decay * cxt + _dot_tn(vx, kw)
        m_ref[h] = jnp.broadcast_to(m_new, m_ref.shape[1:])

        hn_t = h_t * lax.rsqrt(jnp.mean(h_t * h_t, axis=0, keepdims=True) + EPS)
        out_ref[:, sl] = (hn_t.T * nw_ref[:, sl] * so_ref[:, sl].astype(F32)).astype(BF16)


def _mlstm_call(qm, km, vm, so, gates, nw, bsz, seq):
    L = M_CHUNK
    grp = math.gcd(bsz, M_GROUP)
    blk = lambda w: pl.BlockSpec((grp, L, w), lambda b, c: (b, c, 0))
    seqs = lambda a: a.reshape(bsz, seq, a.shape[-1])
    out = pl.pallas_call(
        functools.partial(_mlstm_kernel, grp),
        grid=(bsz // grp, seq // L),
        in_specs=[blk(M_WIDTH), blk(M_WIDTH), blk(M_WIDTH), blk(M_WIDTH), blk(LANES),
                  pl.BlockSpec((1, M_WIDTH), lambda b, c: (0, 0))],
        out_specs=blk(M_WIDTH),
        out_shape=jax.ShapeDtypeStruct((bsz, seq, M_WIDTH), BF16),
        scratch_shapes=[pltpu.VMEM((grp, M_HEADS, 2 * M_HEAD_DIM, M_HEAD_DIM), F32),
                        pltpu.VMEM((grp, M_HEADS, SUBLANES, LANES), F32)],
        compiler_params=pltpu.CompilerParams(dimension_semantics=("arbitrary", "arbitrary"),
                                             vmem_limit_bytes=VMEM_LIMIT),
    )(seqs(qm), seqs(km), seqs(vm), seqs(so), seqs(gates), nw)
    return out.reshape(bsz * seq, M_WIDTH)


def _attn_kernel(ii_ref, jj_ref, q_ref, k_ref, v_ref, lam_ref, sw_ref, nw_ref, out_ref,
                 vt_ref, kp_ref, qa_ref, acc_ref, l_ref, m_ref):
    seq = q_ref.shape[0]
    t = A_BLOCK
    nq = seq // t
    lp = lam_ref[...]
    lam = (jnp.exp(jnp.sum(lp[0:1] * lp[1:2], axis=-1, keepdims=True))
           - jnp.exp(jnp.sum(lp[2:3] * lp[3:4], axis=-1, keepdims=True)) + LAM_INIT)
    for jb in range(nq):
        vt_ref[jb] = v_ref[jb * t:(jb + 1) * t, :].T

    nw = jnp.max(jnp.abs(nw_ref[...]), axis=-1, keepdims=True)
    bound = (LOG2_E * D_HEAD_DIM ** 0.5 * BOUND_SLACK) * nw[0:1] * nw[1:2]
    fixed_shift = jnp.max(bound) <= MAX_FIXED_SHIFT
    bound = jnp.where(fixed_shift, bound, 0.0)

    aux = (D_HEAD_DIM, 0)
    k = k_ref[...]
    q = q_ref[...]
    lane_s = lax.broadcasted_iota(jnp.int32, (seq, LANES), 1)
    in_map_s = (lane_s < D_HEAD_DIM, lane_s >= D_HEAD_DIM)
    for mp in range(2):
        kp_ref[mp] = jnp.where(in_map_s[mp], k, jnp.where(lane_s == aux[mp], 1.0, 0.0).astype(BF16))
        qa_ref[mp] = jnp.where(in_map_s[mp], q, jnp.where(lane_s == aux[mp], -bound, 0.0).astype(BF16))
    l_ref[...] = jnp.zeros(l_ref.shape, F32)
    acc_ref[...] = jnp.zeros(acc_ref.shape, F32)

    k_i = lax.broadcasted_iota(jnp.int32, (t, t), 0)
    q_i = lax.broadcasted_iota(jnp.int32, (t, t), 1)
    diag_mask = k_i <= q_i

    def step(i, j, masked):
        koff = pl.multiple_of(j * t, t)
        qoff = pl.multiple_of(i * t, t)
        vt = vt_ref[j]
        for mp in range(2):
            s = _dot_nt(kp_ref[mp, pl.ds(koff, t), :], qa_ref[mp, pl.ds(qoff, t), :])
            if masked:
                s = jnp.where(diag_mask, s, NEG_INF)
            p = jnp.exp2(s)
            l_ref[mp * nq + i] += jnp.sum(p, axis=0, keepdims=True)
            acc_ref[mp * nq + i] += _dot(vt, p.astype(BF16))

    def step_running_max(i, j, masked):
        koff = pl.multiple_of(j * t, t)
        qoff = pl.multiple_of(i * t, t)
        vt = vt_ref[j]
        for mp in range(2):
            s = _dot_nt(kp_ref[mp, pl.ds(koff, t), :], qa_ref[mp, pl.ds(qoff, t), :])
            if masked:
                s = jnp.where(diag_mask, s, NEG_INF)
            m_old = m_ref[mp * nq + i]
            m_new = jnp.maximum(m_old, jnp.max(s, axis=0, keepdims=True))
            m_safe = jnp.where(m_new == NEG_INF, 0.0, m_new)
            alpha = jnp.exp2(m_old - m_safe)
            p = jnp.exp2(s - m_safe)
            l_ref[mp * nq + i] = alpha * l_ref[mp * nq + i] + jnp.sum(p, axis=0, keepdims=True)
            acc_ref[mp * nq + i] = alpha * acc_ref[mp * nq + i] + _dot(vt, p.astype(BF16))
            m_ref[mp * nq + i] = m_new

    def run(n_steps, step_at, unroll):
        trips = n_steps // unroll

        def body(u, c):
            for r in range(unroll):
                step_at(u * unroll + r)
            return c

        if trips:
            lax.fori_loop(0, trips, body, 0)
        for p in range(trips * unroll, n_steps):
            step_at(p)

    n_pairs = nq * (nq - 1) // 2

    @pl.when(fixed_shift)
    def _():
        run(n_pairs, lambda p: step(ii_ref[p], jj_ref[p], False), A_UNROLL)
        run(nq, lambda p: step(p, p, True), A_UNROLL)

    @pl.when(jnp.logical_not(fixed_shift))
    def _():
        m_ref[...] = jnp.full(m_ref.shape, NEG_INF, F32)
        run(n_pairs, lambda p: step_running_max(ii_ref[p], jj_ref[p], False), 1)
        run(nq, lambda p: step_running_max(p, p, True), 1)

    def finish(i, c):
        o_t = (acc_ref[i] * (1.0 / l_ref[i]) - lam * (acc_ref[nq + i] * (1.0 / l_ref[nq + i])))
        nrm = lax.rsqrt(jnp.mean(o_t * o_t, axis=0, keepdims=True) + EPS)
        on = (o_t * nrm).T * sw_ref[...] * (1.0 - LAM_INIT)
        out_ref[pl.ds(pl.multiple_of(i * t, t), t), :] = on.astype(BF16)
        return c

    lax.fori_loop(0, nq, finish, 0)


def _attn_call(qd, kd, vd, lam_rows, subw, qk_nw, bsz, seq):
    t = A_BLOCK
    nq = seq // t
    pairs = [(i, j) for i in range(nq) for j in range(i)]
    ii = jnp.asarray([p[0] for p in pairs] or [0], jnp.int32)
    jj = jnp.asarray([p[1] for p in pairs] or [0], jnp.int32)
    blk = pl.BlockSpec((seq, LANES), lambda b, h, *_: (b, h))
    const = lambda shape: pl.BlockSpec(shape, lambda b, h, *_: (0,) * len(shape))
    return pl.pallas_call(
        _attn_kernel,
        grid_spec=pltpu.PrefetchScalarGridSpec(
            num_scalar_prefetch=2,
            grid=(bsz, D_HEADS),
            in_specs=[blk, blk, blk, const(lam_rows.shape), const(subw.shape), const(qk_nw.shape)],
            out_specs=blk,
            scratch_shapes=[pltpu.VMEM((nq, D_V_DIM, t), BF16),
                            pltpu.VMEM((2, seq, LANES), BF16),
                            pltpu.VMEM((2, seq, LANES), BF16),
                            pltpu.VMEM((2 * nq, D_V_DIM, t), F32),
                            pltpu.VMEM((2 * nq, 1, t), F32),
                            pltpu.VMEM((2 * nq, 1, t), F32)]),
        out_shape=jax.ShapeDtypeStruct((bsz * seq, D_WIDTH), BF16),
        compiler_params=pltpu.CompilerParams(dimension_semantics=("arbitrary", "arbitrary"),
                                             vmem_limit_bytes=VMEM_LIMIT),
    )(ii, jj, qd, kd, vd, lam_rows, subw, qk_nw)


def _merge_kernel(tiles_per_seq, x_ref, hm_ref, hd_ref, sgm_ref, sgd_ref, mod_ref, wbm_ref, wbd_ref, wout_ref,
                  n2w_ref, wr_hi_ref, wr_lo_ref, br_ref, x1_ref, h2_ref, dest_ref, rw_ref, cnt_ref, carry_ref):
    gate1 = mod_ref[0, 2:3, :]
    shift2 = mod_ref[0, 3:4, :]
    scale2 = mod_ref[0, 4:5, :]
    tm = x_ref.shape[0]
    rows = tm // MERGE_SPLIT
    parts = []
    for part in range(MERGE_SPLIT):
        rs = slice(part * rows, (part + 1) * rows)
        merged = (sgm_ref[rs, :].astype(F32) * _dot(hm_ref[rs, :], wbm_ref[...])
                  + sgd_ref[rs, :].astype(F32) * _dot(hd_ref[rs, :], wbd_ref[...]))
        x1 = x_ref[rs, :] + gate1 * _dot(merged.astype(BF16), wout_ref[...])
        ms = jnp.mean(x1 * x1, axis=-1, keepdims=True)
        h2 = (x1 * lax.rsqrt(ms + EPS) * n2w_ref[...]) * (1.0 + scale2) + shift2
        for ch in range(D_MODEL // LANES):
            cols = slice(ch * LANES, (ch + 1) * LANES)
            x1_ref[pl.ds(part * rows * SUBLANES + ch, rows, stride=SUBLANES), :] = x1[:, cols]
            h2_ref[pl.ds(part * rows * SUBLANES + ch, rows, stride=SUBLANES), :] = h2[:, cols]
        h2_hi = h2.astype(BF16)
        h2_lo = (h2 - h2_hi.astype(F32)).astype(BF16)
        parts.append(_dot(h2_hi, wr_hi_ref[...]) + _dot(h2_lo, wr_hi_ref[...]) + _dot(h2_hi, wr_lo_ref[...]))
    logits = (jnp.concatenate(parts, axis=0) + br_ref[...]).T[0:ROUTE_ROWS, :]
    row = lax.broadcasted_iota(jnp.int32, logits.shape, 0)
    row_f = row.astype(F32)
    big = float(LANES)

    def top(vals):
        mx = jnp.max(vals, axis=0, keepdims=True)
        idx = jnp.min(jnp.where(vals == mx, row_f, big), axis=0, keepdims=True)
        return mx, idx

    is_group = row < N_GROUPS
    gmax, gidx = top(jnp.where(is_group, logits, NEG_INF))
    gsum = jnp.sum(jnp.where(is_group, jnp.exp(logits - gmax), 0.0), axis=0, keepdims=True)
    pg_top = 1.0 / gsum
    lo = N_GROUPS + EXPERTS_PER_GROUP * gidx
    in_group = (row_f >= lo) & (row_f < lo + EXPERTS_PER_GROUP)
    el = jnp.where(in_group, logits, NEG_INF)
    e1max, e1idx = top(el)
    e2max, e2idx = top(jnp.where(row_f == e1idx, NEG_INF, el))
    p2 = jnp.exp(e2max - e1max)
    w1 = pg_top / (1.0 + p2)
    w2 = w1 * p2

    @pl.when(pl.program_id(0) % tiles_per_seq == 0)
    def _():
        carry_ref[...] = jnp.zeros(carry_ref.shape, F32)

    is_e1 = row_f == e1idx
    is_e2 = row_f == e2idx
    onehot = jnp.where(is_e1 | is_e2, 1.0, 0.0)
    r_i = lax.broadcasted_iota(jnp.int32, (tm, tm), 0)
    c_i = lax.broadcasted_iota(jnp.int32, (tm, tm), 1)
    before = jnp.where(r_i < c_i, 1.0, 0.0).astype(BF16)
    cum = _dot(onehot.astype(BF16), before) + carry_ref[...]
    r1 = jnp.sum(jnp.where(is_e1, cum, 0.0), axis=0, keepdims=True)
    r2 = jnp.sum(jnp.where(is_e2, cum, 0.0), axis=0, keepdims=True)
    carry = carry_ref[...] + jnp.sum(onehot, axis=1, keepdims=True)
    carry_ref[...] = carry
    cnt_ref[0] = carry

    seq_len = float(tiles_per_seq * tm)
    row8 = lax.broadcasted_iota(jnp.int32, (SUBLANES, tm), 0)
    pos = jnp.where(row8 == 0, (e1idx - N_GROUPS) * seq_len + r1, (e2idx - N_GROUPS) * seq_len + r2)
    dest_ref[0] = pos.astype(jnp.int32)
    rw_ref[0] = jnp.where(row8 == 0, w1, w2)


def _merge_call(x2, hm, hd, sgm, sgd, mod3, wbm, wbd, wout, n2w, wr_hi, wr_lo, br, seq):
    n, d = x2.shape
    tm = TM_MERGE
    tiles_per_seq = seq // tm
    row = lambda w: pl.BlockSpec((tm, w), lambda i: (i, 0))
    consts = [wbm, wbd, wout, n2w, wr_hi, wr_lo, br]
    per_seq = pl.BlockSpec((1, SUBLANES, tm), lambda i: (i // tiles_per_seq, 0, i % tiles_per_seq))
    return pl.pallas_call(
        functools.partial(_merge_kernel, tiles_per_seq),
        grid=(n // tm,),
        in_specs=[row(d), row(M_WIDTH), row(D_WIDTH), row(d), row(d),
                  pl.BlockSpec((1, 6, d), lambda i: (i // tiles_per_seq, 0, 0))]
                 + [_const_spec(a.shape) for a in consts],
        out_specs=[pl.BlockSpec((tm * SUBLANES, LANES), lambda i: (i, 0)),
                   pl.BlockSpec((tm * SUBLANES, LANES), lambda i: (i, 0)),
                   per_seq, per_seq, pl.BlockSpec((1, ROUTE_ROWS, 1), lambda i: (i, 0, 0))],
        out_shape=[jax.ShapeDtypeStruct((n * SUBLANES, LANES), F32), jax.ShapeDtypeStruct((n * SUBLANES, LANES), F32),
                   jax.ShapeDtypeStruct((n // seq, SUBLANES, seq), jnp.int32),
                   jax.ShapeDtypeStruct((n // seq, SUBLANES, seq), F32),
                   jax.ShapeDtypeStruct((n // tm, ROUTE_ROWS, 1), F32)],
        scratch_shapes=[pltpu.VMEM((ROUTE_ROWS, 1), F32)],
        compiler_params=pltpu.CompilerParams(dimension_semantics=("arbitrary",),
                                             vmem_limit_bytes=VMEM_LIMIT),
    )(x2, hm, hd, sgm, sgd, mod3, *consts)


def _moe_kernel(dest_hbm, rw_hbm, meta_hbm, h2_hbm, x1_hbm, mod_ref, wg_hbm, wu_hbm, wd_hbm, out_hbm,
                dest, rw, meta, inv, h2buf, acc, wg_buf, wu_buf, wd_buf, xst, yst, ost, sems):
    b = pl.program_id(0)
    t_blk = h2buf.shape[0] // SUBLANES - 1
    spare = 2 * t_blk
    r = MOE_R
    tile = lambda ref, tok: ref.at[pl.ds(pl.multiple_of(tok * SUBLANES, SUBLANES), SUBLANES), :]

    def weight_copies(e, slot):
        return (pltpu.make_async_copy(wg_hbm.at[e], wg_buf.at[slot], sems.at[5 + slot]),
                pltpu.make_async_copy(wu_hbm.at[e], wu_buf.at[slot], sems.at[7 + slot]),
                pltpu.make_async_copy(wd_hbm.at[e], wd_buf.at[slot], sems.at[9 + slot]))

    route_copies = (pltpu.make_async_copy(dest_hbm.at[b], dest, sems.at[0]),
                    pltpu.make_async_copy(rw_hbm.at[b], rw, sems.at[1]),
                    pltpu.make_async_copy(meta_hbm.at[b], meta, sems.at[2]))
    h2_copy = pltpu.make_async_copy(h2_hbm.at[b], h2buf.at[pl.ds(0, t_blk * SUBLANES)], sems.at[3])
    x1_copy = pltpu.make_async_copy(x1_hbm.at[b], acc.at[pl.ds(0, t_blk * SUBLANES)], sems.at[4])

    for cp in route_copies + (h2_copy, x1_copy) + weight_copies(0, 0):
        cp.start()
    xst[...] = jnp.zeros(xst.shape, F32)
    yst[...] = jnp.zeros(yst.shape, F32)
    tile(h2buf, t_blk)[...] = jnp.zeros((SUBLANES, LANES), F32)
    tile(acc, t_blk)[...] = jnp.zeros((SUBLANES, LANES), F32)
    for cp in route_copies:
        cp.wait()

    last = inv.shape[0] - 1

    def pad_body(e, c):
        for j in range(MOE_U - 1):
            inv[jnp.minimum(e * t_blk + meta[e] + j, last)] = spare
        return c

    lax.fori_loop(0, N_EXPERTS, pad_body, 0)

    def inv_body(u, c):
        for j in range(MOE_U):
            s = u * MOE_U + j
            inv[dest[s]] = s
        return c

    lax.fori_loop(0, spare // MOE_U, inv_body, 0)
    h2_copy.wait()
    x1_copy.wait()
    gate2 = mod_ref[0, 5:6, :]
    n_chunks = D_MODEL // LANES

    def expert_body(e, c):
        slot = e % 2
        for cp in weight_copies(e, slot):
            cp.wait()

        @pl.when(e + 1 < N_EXPERTS)
        def _():
            for cp in weight_copies(e + 1, 1 - slot):
                cp.start()

        seg = e * t_blk
        n = meta[e]

        def run_tile(base, count, rows):
            groups = (count + MOE_U - 1) // MOE_U

            def gather(u, c3):
                row0 = u * MOE_U
                pos0 = base + row0
                for j in range(MOE_U):
                    xst[pl.ds(row0 + j, SUBLANES, stride=MOE_RS), :] = tile(h2buf, inv[pos0 + j] >> 1)[...]
                return c3

            lax.fori_loop(0, groups, gather, 0)
            x = jnp.concatenate([xst[ch * MOE_RS:ch * MOE_RS + rows, :] for ch in range(n_chunks)], axis=1)
            xb = x.astype(BF16)
            g = _dot(xb, wg_buf[slot])
            act = (g * _sigmoid(g) * _dot(xb, wu_buf[slot])).astype(BF16)
            y = _dot(act, wd_buf[slot]) * gate2
            for ch in range(n_chunks):
                yst[ch * MOE_RS:ch * MOE_RS + rows, :] = y[:, ch * LANES:(ch + 1) * LANES]

            def scatter(u, c3):
                toks, vals = [], []
                row0 = u * MOE_U
                pos0 = base + row0
                for j in range(MOE_U):
                    v = inv[pos0 + j]
                    toks.append(v >> 1)
                    vals.append(tile(acc, v >> 1)[...]
                                + rw[v] * yst[pl.ds(row0 + j, SUBLANES, stride=MOE_RS), :])
                for tok, val in zip(toks, vals):
                    tile(acc, tok)[...] = val
                return c3

            lax.fori_loop(0, groups, scatter, 0)

        def full_tile(k, c2):
            run_tile(seg + k * r, r, r)
            return c2

        n_full = n // r
        lax.fori_loop(0, n_full, full_tile, 0)
        rem = n - n_full * r

        @pl.when(rem > r // 2)
        def _():
            run_tile(seg + n_full * r, rem, r)

        @pl.when((rem > 0) & (rem <= r // 2))
        def _():
            run_tile(seg + n_full * r, rem, r // 2)

        return c

    lax.fori_loop(0, N_EXPERTS, expert_body, 0)

    rows = ost.shape[1]
    n_pieces = t_blk // rows

    def out_copy(p):
        return pltpu.make_async_copy(ost.at[p % 2], out_hbm.at[b, pl.ds(p * rows, rows)], sems.at[11 + p % 2])

    for p in range(n_pieces):
        if p >= 2:
            out_copy(p - 2).wait()
        for ch in range(n_chunks):
            ost[p % 2, :, ch * LANES:(ch + 1) * LANES] = acc[pl.ds(p * rows * SUBLANES + ch, rows, stride=SUBLANES), :]
        out_copy(p).start()
    for p in range(max(0, n_pieces - 2), n_pieces):
        out_copy(p).wait()


def _moe_call(dest, rw, meta, h2, x1, mod3, wg, wu, wd):
    nb = h2.shape[0]
    t_blk = h2.shape[1] // SUBLANES
    d = D_MODEL
    tok_rows = (t_blk + 1) * SUBLANES
    any_spec = pl.BlockSpec(memory_space=pl.ANY)
    return pl.pallas_call(
        _moe_kernel,
        grid=(nb,),
        in_specs=[any_spec, any_spec, any_spec, any_spec, any_spec,
                  pl.BlockSpec((1, 6, d), lambda i: (i, 0, 0)), any_spec, any_spec, any_spec],
        out_specs=any_spec,
        out_shape=jax.ShapeDtypeStruct((nb, t_blk, d), F32),
        scratch_shapes=[pltpu.SMEM(dest.shape[1:], jnp.int32), pltpu.SMEM(rw.shape[1:], F32),
                        pltpu.SMEM(meta.shape[1:], jnp.int32), pltpu.SMEM((N_EXPERTS * t_blk,), jnp.int32),
                        pltpu.VMEM((tok_rows, LANES), F32), pltpu.VMEM((tok_rows, LANES), F32),
                        pltpu.VMEM((2, d, D_EXPERT), BF16), pltpu.VMEM((2, d, D_EXPERT), BF16),
                        pltpu.VMEM((2, D_EXPERT, d), BF16),
                        pltpu.VMEM((SUBLANES * MOE_RS, LANES), F32), pltpu.VMEM((SUBLANES * MOE_RS, LANES), F32),
                        pltpu.VMEM((2, min(MOE_OUT_ROWS, t_blk), d), F32),
                        pltpu.SemaphoreType.DMA((13,))],
        compiler_params=pltpu.CompilerParams(dimension_semantics=("arbitrary",),
                                             vmem_limit_bytes=VMEM_LIMIT),
    )(dest, rw, meta, h2, x1, mod3, wg, wu, wd)


def _rope_freq_row():
    inv = ROPE_THETA ** (-np.arange(0, ROPE_DIM, 2, dtype=np.float32) / ROPE_DIM)
    row = np.zeros((LANES,), np.float32)
    half = ROPE_DIM // 2
    for base in range(0, LANES, D_HEAD_DIM):
        row[base:base + half] = inv
        row[base + half:base + ROPE_DIM] = inv
    return jnp.asarray(row.reshape(1, LANES))


def _block_diag_ones():
    idx = np.arange(MXU_DIM) // D_HEAD_DIM
    return jnp.asarray((idx[:, None] == idx[None, :]).astype(np.float32), dtype=BF16)


def kernel(x, c, positions, w_ada, b_ada, norm1_w, w_in, b_igate, b_fgate, conv_w, conv_b, mlstm_norm_w, q_norm_w, k_norm_w, lam_q1, lam_k1, lam_q2, lam_k2, subln_w, w_br_m, w_br_d, w_out, norm2_w, w_rg, b_rg, w_re, b_re, w_gate, w_up, w_down):
    bsz, seq, d = x.shape
    n = bsz * seq
    x2 = x.reshape(n, d)
    pos2 = positions.reshape(n, 1)
    l = 0

    mod3 = _ada_call(c, w_ada[l], b_ada[l]).reshape(bsz, 6, d)

    wi = w_in[l]
    gate_lo = 4 * M_WIDTH
    gate_hi = gate_lo + 2 * M_HEADS
    w_packed = jnp.concatenate(
        [wi[:, :gate_lo].astype(BF16),
         jnp.pad(wi[:, gate_lo:gate_hi], ((0, 0), (0, LANES - 2 * M_HEADS))).astype(BF16),
         wi[:, gate_hi:].astype(BF16)], axis=1)
    gate_bias = jnp.pad(jnp.concatenate([b_igate[l], b_fgate[l]]), (0, LANES - 2 * M_HEADS)).reshape(1, LANES)
    reps = D_WIDTH // D_HEAD_DIM
    smalls = [conv_w[l], conv_b[l].reshape(1, -1), gate_bias,
              jnp.tile(q_norm_w[l], reps).reshape(1, D_WIDTH), jnp.tile(k_norm_w[l], reps).reshape(1, D_WIDTH),
              _rope_freq_row(), _block_diag_ones()]
    qm, km, vm, so, gates, qd, kd, vd, sgm, sgd = _in_call(
        x2, pos2, mod3, norm1_w[l].reshape(1, d), w_packed, smalls, seq)

    hm = _mlstm_call(qm, km, vm, so, gates, mlstm_norm_w[l].reshape(1, M_WIDTH), bsz, seq)

    lam_rows = jnp.stack([lam_q1[l], lam_k1[l], lam_q2[l], lam_k2[l]])
    hd = _attn_call(qd, kd, vd, lam_rows, subln_w[l].reshape(1, D_V_DIM), jnp.stack([q_norm_w[l], k_norm_w[l]]),
                    bsz, seq)

    w_r = jnp.pad(jnp.concatenate([w_rg[l], w_re[l]], axis=1), ((0, 0), (0, LANES - N_GROUPS - N_EXPERTS)))
    wr_hi = w_r.astype(BF16)
    wr_lo = (w_r - wr_hi.astype(F32)).astype(BF16)
    b_r = jnp.pad(jnp.concatenate([b_rg[l], b_re[l]]), (0, LANES - N_GROUPS - N_EXPERTS)).reshape(1, LANES)
    x1, h2, dest, rw, cnt = _merge_call(x2, hm, hd, sgm, sgd, mod3, w_br_m[l].astype(BF16), w_br_d[l].astype(BF16),
                                        w_out[l].astype(BF16), norm2_w[l].reshape(1, d), wr_hi, wr_lo, b_r, seq)

    tiles_per_seq = seq // TM_MERGE
    meta = cnt[tiles_per_seq - 1::tiles_per_seq, N_GROUPS:N_GROUPS + N_EXPERTS, 0].astype(jnp.int32)
    meta = jnp.pad(meta, ((0, 0), (0, LANES - N_EXPERTS)))
    interleave = lambda a: a[:, 0:2, :].transpose(0, 2, 1).reshape(bsz, 2 * seq)
    dest = interleave(dest)
    rw = jnp.pad(interleave(rw), ((0, 0), (0, LANES)))
    tok_tiles = (bsz, seq * SUBLANES, LANES)
    return _moe_call(dest, rw, meta, h2.reshape(tok_tiles), x1.reshape(tok_tiles), mod3,
                     w_gate[l].astype(BF16), w_up[l].astype(BF16), w_down[l].astype(BF16))
```

```python
import functools
import math

import numpy as np
import jax
import jax.numpy as jnp
from jax import lax
from jax.experimental import pallas as pl
from jax.experimental.pallas import tpu as pltpu

F32 = jnp.float32
BF16 = jnp.bfloat16

D_MODEL = 1024
M_HEADS = 4
M_HEAD_DIM = 128
M_WIDTH = M_HEADS * M_HEAD_DIM
CONV_WIDTH = 4
D_HEADS = 4
D_HEAD_DIM = 64
D_V_DIM = 2 * D_HEAD_DIM
D_WIDTH = D_HEADS * D_V_DIM
ROPE_THETA = 500000.0
ROPE_DIM = D_HEAD_DIM // 4
N_GROUPS = 4
EXPERTS_PER_GROUP = 8
N_EXPERTS = N_GROUPS * EXPERTS_PER_GROUP
D_EXPERT = D_MODEL // 4
EPS = 1e-6
LAM_INIT = 0.8 - 0.6 * math.exp(-0.3 * 0)

LANES = 128
SUBLANES = 8
MXU_DIM = 256
VMEM_LIMIT = 56 * 1024 * 1024

TM_IN = 512
IN_SPLIT = 1
MERGE_SPLIT = 1
ROUTE_ROWS = 40
TM_MERGE = 512
M_CHUNK = 256
M_GROUP = 4
A_BLOCK = 512
LOG2_E = math.log2(math.e)
A_UNROLL = 7
BOUND_SLACK = 1.01
MAX_FIXED_SHIFT = 60.0
MOE_R = 256
MOE_RS = MOE_R + SUBLANES
MOE_U = 16
MOE_OUT_ROWS = 512
IN_SEGMENTS = (("qk", 2 * M_WIDTH), ("v", M_WIDTH), ("o", M_WIDTH), ("g", LANES),
               ("qd", D_WIDTH), ("kd", D_WIDTH), ("vd", D_WIDTH), ("gm", D_MODEL), ("gd", D_MODEL))
NEG_INF = float("-inf")


def _dot(a, b):
    return jnp.dot(a, b, preferred_element_type=F32)


def _dot_nt(a, b):
    return lax.dot_general(a, b, (((1,), (1,)), ((), ())), preferred_element_type=F32)


def _dot_tn(a, b):
    return lax.dot_general(a, b, (((0,), (0,)), ((), ())), preferred_element_type=F32)


def _sigmoid(v):
    return 0.5 * jnp.tanh(0.5 * v) + 0.5


def _const_spec(shape):
    nd = len(shape)
    return pl.BlockSpec(shape, lambda *_: (0,) * nd)


def _ada_kernel(c_ref, w_ref, b_ref, o_ref):
    o_ref[...] = _dot(c_ref[...].astype(BF16), w_ref[...].astype(BF16)) + b_ref[...]


def _ada_call(c, w_ada, b_ada):
    bsz, d = c.shape
    cols = w_ada.shape[1]
    tn = 1024
    return pl.pallas_call(
        _ada_kernel,
        grid=(cols // tn,),
        in_specs=[pl.BlockSpec((bsz, d), lambda j: (0, 0)),
                  pl.BlockSpec((d, tn), lambda j: (0, j)),
                  pl.BlockSpec((1, tn), lambda j: (0, j))],
        out_specs=pl.BlockSpec((bsz, tn), lambda j: (0, j)),
        out_shape=jax.ShapeDtypeStruct((bsz, cols), F32),
        compiler_params=pltpu.CompilerParams(dimension_semantics=("arbitrary",)),
    )(c, w_ada, b_ada.reshape(1, cols))


def _in_kernel(tiles_per_seq,
               x_ref, pos_ref, mod_ref, n1w_ref,
               w_ref, cw_ref, cb_ref, gb_ref, qnw_ref, knw_ref, freq_ref, bd_ref,
               qm_ref, km_ref, vm_ref, so_ref, gates_ref, qd_ref, kd_ref, vd_ref, sgm_ref, sgd_ref,
               cbuf):
    tm = x_ref.shape[0]
    i = pl.program_id(0)
    seg, o = {}, 0
    for name, width in IN_SEGMENTS:
        seg[name] = w_ref.at[:, o:o + width]
        o += width

    @pl.when(i == 0)
    def _():
        cbuf[...] = jnp.zeros(cbuf.shape, F32)

    prev = jnp.where(i % tiles_per_seq == 0, 0.0, cbuf[...])
    rows = tm // IN_SPLIT
    for part in range(IN_SPLIT):
        prev = _in_rows(slice(part * rows, (part + 1) * rows), prev,
                        x_ref, pos_ref, mod_ref, n1w_ref, seg, cw_ref, cb_ref, gb_ref, qnw_ref, knw_ref, freq_ref,
                        bd_ref, qm_ref, km_ref, vm_ref, so_ref, gates_ref, qd_ref, kd_ref, vd_ref, sgm_ref, sgd_ref)
    cbuf[...] = prev


def _in_rows(rs, prev, x_ref, pos_ref, mod_ref, n1w_ref, seg, cw_ref, cb_ref, gb_ref, qnw_ref, knw_ref, freq_ref,
             bd_ref, qm_ref, km_ref, vm_ref, so_ref, gates_ref, qd_ref, kd_ref, vd_ref, sgm_ref, sgd_ref):
    wqk_ref, wv_ref, wo_ref, wg_ref = seg["qk"], seg["v"], seg["o"], seg["g"]
    wqd_ref, wkd_ref, wvd_ref, wgm_ref, wgd_ref = seg["qd"], seg["kd"], seg["vd"], seg["gm"], seg["gd"]
    x = x_ref[rs, :]
    tm = x.shape[0]
    ms = jnp.mean(x * x, axis=-1, keepdims=True)
    scale1 = mod_ref[0, 1:2, :]
    shift1 = mod_ref[0, 0:1, :]
    h = (x * lax.rsqrt(ms + EPS) * n1w_ref[...]) * (1.0 + scale1) + shift1
    hb = h.astype(BF16)

    y = _dot(hb, wqk_ref[...])
    row8 = lax.broadcasted_iota(jnp.int32, (SUBLANES, 2 * M_WIDTH), 0)
    acc = cb_ref[...] + cw_ref[3:4, :] * y
    for d in range(1, CONV_WIDTH):
        shifted = pltpu.roll(y, d, 0)
        head = jnp.where(row8 < d, pltpu.roll(prev, d, 0), shifted[:SUBLANES])
        shifted = jnp.concatenate([head, shifted[SUBLANES:]], axis=0)
        acc = acc + cw_ref[3 - d:4 - d, :] * shifted
    a = acc * _sigmoid(acc)
    qm_ref[rs, :] = a[:, :M_WIDTH].astype(BF16)
    km_ref[rs, :] = (a[:, M_WIDTH:] * (M_HEAD_DIM ** -0.5)).astype(BF16)

    sgm_ref[rs, :] = _sigmoid(_dot(hb, wgm_ref[...])).astype(BF16)
    vm_ref[rs, :] = _dot(hb, wv_ref[...]).astype(BF16)
    so_ref[rs, :] = _sigmoid(_dot(hb, wo_ref[...])).astype(BF16)

    g = _dot(hb, wg_ref[...]) + gb_ref[...]
    lane = lax.broadcasted_iota(jnp.int32, g.shape, 1)
    logsig = jnp.minimum(g, 0.0) - jnp.log(1.0 + jnp.exp(-jnp.abs(g)))
    gates_ref[rs, :] = jnp.where(lane < M_HEADS, g, logsig)

    ang = pos_ref[rs, :].astype(F32) * freq_ref[...]
    cos = jnp.cos(ang)
    sin = jnp.sin(ang)
    lane_in_map = lax.broadcasted_iota(jnp.int32, ang.shape, 1) % D_HEAD_DIM
    half = ROPE_DIM // 2
    s_from_lo = jnp.where(lane_in_map >= half, sin, 0.0)
    s_from_hi = jnp.where(lane_in_map < half, -sin, 0.0)
    reps = D_WIDTH // LANES
    cos_w = jnp.concatenate([cos] * reps, axis=1)
    slo_w = jnp.concatenate([s_from_lo] * reps, axis=1)
    shi_w = jnp.concatenate([s_from_hi] * reps, axis=1)

    def qk_norm_rope(w_ref, nw_ref, out_scale):
        t = _dot(hb, w_ref[...])
        sq = (t * t).astype(BF16)
        gs = [_dot(sq[:, c * MXU_DIM:(c + 1) * MXU_DIM], bd_ref[...]) for c in range(D_WIDTH // MXU_DIM)]
        msq = jnp.concatenate(gs, axis=1) * (1.0 / D_HEAD_DIM)
        tn = t * lax.rsqrt(msq + EPS) * nw_ref[...]
        rot = (tn * cos_w + pltpu.roll(tn, half, 1) * slo_w
               + pltpu.roll(tn, D_WIDTH - half, 1) * shi_w)
        return (rot * out_scale).astype(BF16)

    qd_ref[rs, :] = qk_norm_rope(wqd_ref, qnw_ref, D_HEAD_DIM ** -0.5 * LOG2_E)
    sgd_ref[rs, :] = _sigmoid(_dot(hb, wgd_ref[...])).astype(BF16)
    kd_ref[rs, :] = qk_norm_rope(wkd_ref, knw_ref, 1.0)
    vd_ref[rs, :] = _dot(hb, wvd_ref[...]).astype(BF16)
    return y[tm - SUBLANES:, :]


def _in_call(x2, pos2, mod3, n1w, w_packed, smalls, seq):
    n, d = x2.shape
    tm = TM_IN
    tiles_per_seq = seq // tm
    row = lambda w: pl.BlockSpec((tm, w), lambda i: (i, 0))
    in_specs = [row(d), row(1),
                pl.BlockSpec((1, 6, d), lambda i: (i // tiles_per_seq, 0, 0)),
                _const_spec(n1w.shape)]
    in_specs += [_const_spec(s.shape) for s in (w_packed, *smalls)]
    out_widths = [(M_WIDTH, BF16), (M_WIDTH, BF16), (M_WIDTH, BF16), (M_WIDTH, BF16), (LANES, F32),
                  (D_WIDTH, BF16), (D_WIDTH, BF16), (D_WIDTH, BF16), (d, BF16), (d, BF16)]
    return pl.pallas_call(
        functools.partial(_in_kernel, tiles_per_seq),
        grid=(n // tm,),
        in_specs=in_specs,
        out_specs=[row(w) for w, _ in out_widths],
        out_shape=[jax.ShapeDtypeStruct((n, w), dt) for w, dt in out_widths],
        scratch_shapes=[pltpu.VMEM((SUBLANES, 2 * M_WIDTH), F32)],
        compiler_params=pltpu.CompilerParams(dimension_semantics=("arbitrary",),
                                             vmem_limit_bytes=VMEM_LIMIT),
    )(x2, pos2, mod3, n1w, w_packed, *smalls)


def _mlstm_kernel(grp, q_ref, k_ref, v_ref, so_ref, gates_ref, nw_ref, out_ref, cx_ref, m_ref):
    @pl.when(pl.program_id(1) == 0)
    def _():
        cx_ref[...] = jnp.zeros(cx_ref.shape, F32)
        m_ref[...] = jnp.zeros(m_ref.shape, F32)

    for gi in range(grp):
        _mlstm_chunk(q_ref.at[gi], k_ref.at[gi], v_ref.at[gi], so_ref.at[gi], gates_ref.at[gi], nw_ref,
                     out_ref.at[gi], cx_ref.at[gi], m_ref.at[gi])


def _mlstm_chunk(q_ref, k_ref, v_ref, so_ref, gates_ref, nw_ref, out_ref, cx_ref, m_ref):
    L = q_ref.shape[0]
    hd = M_HEAD_DIM
    g = gates_ref[...]
    r_i = lax.broadcasted_iota(jnp.int32, (L, L), 0)
    c_i = lax.broadcasted_iota(jnp.int32, (L, L), 1)
    causal = c_i <= r_i
    tri = causal.astype(BF16)
    g_hi = g.astype(BF16)
    g_mid = (g - g_hi.astype(F32)).astype(BF16)
    g_lo = (g - g_hi.astype(F32) - g_mid.astype(F32)).astype(BF16)
    bc = _dot(tri, g_hi) + _dot(tri, g_mid) + _dot(tri, g_lo)
    g_t = g.T
    bc_t = bc.T
    ones_col = jnp.where(lax.broadcasted_iota(jnp.int32, (L, hd), 1) == 0, 1.0, 0.0).astype(BF16)

    causal_t = r_i <= c_i
    for h in range(M_HEADS):
        sl = slice(h * hd, (h + 1) * hd)
        i_row = g_t[h:h + 1, :]
        b_row = bc_t[M_HEADS + h:M_HEADS + h + 1, :]
        r_col = g[:, h:h + 1] - bc[:, M_HEADS + h:M_HEADS + h + 1]
        m_prev = m_ref[h][0:1, 0:1]
        cxt = cx_ref[h]

        d_t = jnp.where(causal_t, b_row + r_col, NEG_INF)
        m_inter = b_row + m_prev
        m_row = jnp.maximum(m_inter, jnp.max(d_t, axis=0, keepdims=True))
        q = q_ref[:, sl]
        k = k_ref[:, sl]
        vx = jnp.concatenate([v_ref[:, sl], ones_col], axis=1)
        s_t = _dot_nt(k, q) * jnp.exp(d_t - m_row)
        inter = jnp.exp(m_inter - m_row)
        num_t = _dot_tn(vx, s_t.astype(BF16)) + inter * _dot_nt(cxt.astype(BF16), q)
        den = jnp.maximum(jnp.abs(num_t[hd:hd + 1, :]), jnp.exp(-m_row))
        h_t = num_t[:hd, :] * (1.0 / den)

        b_last = b_row[:, L - 1:L]
        m_new = jnp.maximum(b_last + m_prev, jnp.max(b_last + (i_row - b_row), axis=1, keepdims=True))
        wexp = jnp.exp(b_last + r_col - m_new)
        decay = jnp.exp(b_last + m_prev - m_new)
        kw = (k.astype(F32) * wexp).astype(BF16)
        cx_ref[h] = decay * cxt + _dot_tn(vx, kw)
        m_ref[h] = jnp.broadcast_to(m_new, m_ref.shape[1:])

        hn_t = h_t * lax.rsqrt(jnp.mean(h_t * h_t, axis=0, keepdims=True) + EPS)
        out_ref[:, sl] = (hn_t.T * nw_ref[:, sl] * so_ref[:, sl].astype(F32)).astype(BF16)


def _mlstm_call(qm, km, vm, so, gates, nw, bsz, seq):
    L = M_CHUNK
    grp = math.gcd(bsz, M_GROUP)
    blk = lambda w: pl.BlockSpec((grp, L, w), lambda b, c: (b, c, 0))
    seqs = lambda a: a.reshape(bsz, seq, a.shape[-1])
    out = pl.pallas_call(
        functools.partial(_mlstm_kernel, grp),
        grid=(bsz // grp, seq // L),
        in_specs=[blk(M_WIDTH), blk(M_WIDTH), blk(M_WIDTH), blk(M_WIDTH), blk(LANES),
                  pl.BlockSpec((1, M_WIDTH), lambda b, c: (0, 0))],
        out_specs=blk(M_WIDTH),
        out_shape=jax.ShapeDtypeStruct((bsz, seq, M_WIDTH), BF16),
        scratch_shapes=[pltpu.VMEM((grp, M_HEADS, 2 * M_HEAD_DIM, M_HEAD_DIM), F32),
                        pltpu.VMEM((grp, M_HEADS, SUBLANES, LANES), F32)],
        compiler_params=pltpu.CompilerParams(dimension_semantics=("arbitrary", "arbitrary"),
                                             vmem_limit_bytes=VMEM_LIMIT),
    )(seqs(qm), seqs(km), seqs(vm), seqs(so), seqs(gates), nw)
    return out.reshape(bsz * seq, M_WIDTH)


def _attn_kernel(ii_ref, jj_ref, q_ref, k_ref, v_ref, lam_ref, sw_ref, nw_ref, out_ref,
                 vt_ref, kp_ref, qa_ref, acc_ref, l_ref, m_ref):
    seq = q_ref.shape[0]
    t = A_BLOCK
    nq = seq // t
    lp = lam_ref[...]
    lam = (jnp.exp(jnp.sum(lp[0:1] * lp[1:2], axis=-1, keepdims=True))
           - jnp.exp(jnp.sum(lp[2:3] * lp[3:4], axis=-1, keepdims=True)) + LAM_INIT)
    for jb in range(nq):
        vt_ref[jb] = v_ref[jb * t:(jb + 1) * t, :].T

    nw = jnp.max(jnp.abs(nw_ref[...]), axis=-1, keepdims=True)
    bound = (LOG2_E * D_HEAD_DIM ** 0.5 * BOUND_SLACK) * nw[0:1] * nw[1:2]
    fixed_shift = jnp.max(bound) <= MAX_FIXED_SHIFT
    bound = jnp.where(fixed_shift, bound, 0.0)

    aux = (D_HEAD_DIM, 0)
    k = k_ref[...]
    q = q_ref[...]
    lane_s = lax.broadcasted_iota(jnp.int32, (seq, LANES), 1)
    in_map_s = (lane_s < D_HEAD_DIM, lane_s >= D_HEAD_DIM)
    for mp in range(2):
        kp_ref[mp] = jnp.where(in_map_s[mp], k, jnp.where(lane_s == aux[mp], 1.0, 0.0).astype(BF16))
        qa_ref[mp] = jnp.where(in_map_s[mp], q, jnp.where(lane_s == aux[mp], -bound, 0.0).astype(BF16))
    l_ref[...] = jnp.zeros(l_ref.shape, F32)
    acc_ref[...] = jnp.zeros(acc_ref.shape, F32)

    k_i = lax.broadcasted_iota(jnp.int32, (t, t), 0)
    q_i = lax.broadcasted_iota(jnp.int32, (t, t), 1)
    diag_mask = k_i <= q_i

    def step(i, j, masked):
        koff = pl.multiple_of(j * t, t)
        qoff = pl.multiple_of(i * t, t)
        vt = vt_ref[j]
        for mp in range(2):
            s = _dot_nt(kp_ref[mp, pl.ds(koff, t), :], qa_ref[mp, pl.ds(qoff, t), :])
            if masked:
                s = jnp.where(diag_mask, s, NEG_INF)
            p = jnp.exp2(s)
            l_ref[mp * nq + i] += jnp.sum(p, axis=0, keepdims=True)
            acc_ref[mp * nq + i] += _dot(vt, p.astype(BF16))

    def step_running_max(i, j, masked):
        koff = pl.multiple_of(j * t, t)
        qoff = pl.multiple_of(i * t, t)
        vt = vt_ref[j]
        for mp in range(2):
            s = _dot_nt(kp_ref[mp, pl.ds(koff, t), :], qa_ref[mp, pl.ds(qoff, t), :])
            if masked:
                s = jnp.where(diag_mask, s, NEG_INF)
            m_old = m_ref[mp * nq + i]
            m_new = jnp.maximum(m_old, jnp.max(s, axis=0, keepdims=True))
            m_safe = jnp.where(m_new == NEG_INF, 0.0, m_new)
            alpha = jnp.exp2(m_old - m_safe)
            p = jnp.exp2(s - m_safe)
            l_ref[mp * nq + i] = alpha * l_ref[mp * nq + i] + jnp.sum(p, axis=0, keepdims=True)
            acc_ref[mp * nq + i] = alpha * acc_ref[mp * nq + i] + _dot(vt, p.astype(BF16))
            m_ref[mp * nq + i] = m_new

    def run(n_steps, step_at, unroll):
        trips = n_steps // unroll

        def body(u, c):
            for r in range(unroll):
                step_at(u * unroll + r)
            return c

        if trips:
            lax.fori_loop(0, trips, body, 0)
        for p in range(trips * unroll, n_steps):
            step_at(p)

    n_pairs = nq * (nq - 1) // 2

    @pl.when(fixed_shift)
    def _():
        run(n_pairs, lambda p: step(ii_ref[p], jj_ref[p], False), A_UNROLL)
        run(nq, lambda p: step(p, p, True), A_UNROLL)

    @pl.when(jnp.logical_not(fixed_shift))
    def _():
        m_ref[...] = jnp.full(m_ref.shape, NEG_INF, F32)
        run(n_pairs, lambda p: step_running_max(ii_ref[p], jj_ref[p], False), 1)
        run(nq, lambda p: step_running_max(p, p, True), 1)

    def finish(i, c):
        o_t = (acc_ref[i] * (1.0 / l_ref[i]) - lam * (acc_ref[nq + i] * (1.0 / l_ref[nq + i])))
        nrm = lax.rsqrt(jnp.mean(o_t * o_t, axis=0, keepdims=True) + EPS)
        on = (o_t * nrm).T * sw_ref[...] * (1.0 - LAM_INIT)
        out_ref[pl.ds(pl.multiple_of(i * t, t), t), :] = on.astype(BF16)
        return c

    lax.fori_loop(0, nq, finish, 0)


def _attn_call(qd, kd, vd, lam_rows, subw, qk_nw, bsz, seq):
    t = A_BLOCK
    nq = seq // t
    pairs = [(i, j) for i in range(nq) for j in range(i)]
    ii = jnp.asarray([p[0] for p in pairs] or [0], jnp.int32)
    jj = jnp.asarray([p[1] for p in pairs] or [0], jnp.int32)
    blk = pl.BlockSpec((seq, LANES), lambda b, h, *_: (b, h))
    const = lambda shape: pl.BlockSpec(shape, lambda b, h, *_: (0,) * len(shape))
    return pl.pallas_call(
        _attn_kernel,
        grid_spec=pltpu.PrefetchScalarGridSpec(
            num_scalar_prefetch=2,
            grid=(bsz, D_HEADS),
            in_specs=[blk, blk, blk, const(lam_rows.shape), const(subw.shape), const(qk_nw.shape)],
            out_specs=blk,
            scratch_shapes=[pltpu.VMEM((nq, D_V_DIM, t), BF16),
                            pltpu.VMEM((2, seq, LANES), BF16),
                            pltpu.VMEM((2, seq, LANES), BF16),
                            pltpu.VMEM((2 * nq, D_V_DIM, t), F32),
                            pltpu.VMEM((2 * nq, 1, t), F32),
                            pltpu.VMEM((2 * nq, 1, t), F32)]),
        out_shape=jax.ShapeDtypeStruct((bsz * seq, D_WIDTH), BF16),
        compiler_params=pltpu.CompilerParams(dimension_semantics=("arbitrary", "arbitrary"),
                                             vmem_limit_bytes=VMEM_LIMIT),
    )(ii, jj, qd, kd, vd, lam_rows, subw, qk_nw)


def _merge_kernel(tiles_per_seq, x_ref, hm_ref, hd_ref, sgm_ref, sgd_ref, mod_ref, wbm_ref, wbd_ref, wout_ref,
                  n2w_ref, wr_ref, br_ref, x1_ref, h2_ref, dest_ref, rw_ref, cnt_ref, carry_ref):
    gate1 = mod_ref[0, 2:3, :]
    shift2 = mod_ref[0, 3:4, :]
    scale2 = mod_ref[0, 4:5, :]
    tm = x_ref.shape[0]
    rows = tm // MERGE_SPLIT
    parts = []
    for part in range(MERGE_SPLIT):
        rs = slice(part * rows, (part + 1) * rows)
        merged = (sgm_ref[rs, :].astype(F32) * _dot(hm_ref[rs, :], wbm_ref[...])
                  + sgd_ref[rs, :].astype(F32) * _dot(hd_ref[rs, :], wbd_ref[...]))
        x1 = x_ref[rs, :] + gate1 * _dot(merged.astype(BF16), wout_ref[...])
        ms = jnp.mean(x1 * x1, axis=-1, keepdims=True)
        h2 = (x1 * lax.rsqrt(ms + EPS) * n2w_ref[...]) * (1.0 + scale2) + shift2
        for ch in range(D_MODEL // LANES):
            cols = slice(ch * LANES, (ch + 1) * LANES)
            x1_ref[pl.ds(part * rows * SUBLANES + ch, rows, stride=SUBLANES), :] = x1[:, cols]
            h2_ref[pl.ds(part * rows * SUBLANES + ch, rows, stride=SUBLANES), :] = h2[:, cols]
        h2_hi = h2.astype(BF16)
        h2_lo = (h2 - h2_hi.astype(F32)).astype(BF16)
        hh = _dot(h2_hi, wr_ref[...])
        parts.append(hh[:, :LANES] + hh[:, LANES:] + _dot(h2_lo, wr_ref[:, :LANES]))
    logits = (jnp.concatenate(parts, axis=0) + br_ref[...]).T[0:ROUTE_ROWS, :]
    row = lax.broadcasted_iota(jnp.int32, logits.shape, 0)
    row_f = row.astype(F32)
    big = float(LANES)

    def top(vals):
        mx = jnp.max(vals, axis=0, keepdims=True)
        idx = jnp.min(jnp.where(vals == mx, row_f, big), axis=0, keepdims=True)
        return mx, idx

    is_group = row < N_GROUPS
    gmax, gidx = top(jnp.where(is_group, logits, NEG_INF))
    gsum = jnp.sum(jnp.where(is_group, jnp.exp(logits - gmax), 0.0), axis=0, keepdims=True)
    pg_top = 1.0 / gsum
    lo = N_GROUPS + EXPERTS_PER_GROUP * gidx
    in_group = (row_f >= lo) & (row_f < lo + EXPERTS_PER_GROUP)
    el = jnp.where(in_group, logits, NEG_INF)
    e1max, e1idx = top(el)
    e2max, e2idx = top(jnp.where(row_f == e1idx, NEG_INF, el))
    p2 = jnp.exp(e2max - e1max)
    w1 = pg_top / (1.0 + p2)
    w2 = w1 * p2

    @pl.when(pl.program_id(0) % tiles_per_seq == 0)
    def _():
        carry_ref[...] = jnp.zeros(carry_ref.shape, F32)

    is_e1 = row_f == e1idx
    is_e2 = row_f == e2idx
    onehot = jnp.where(is_e1 | is_e2, 1.0, 0.0)
    r_i = lax.broadcasted_iota(jnp.int32, (tm, tm), 0)
    c_i = lax.broadcasted_iota(jnp.int32, (tm, tm), 1)
    before = jnp.where(r_i < c_i, 1.0, 0.0).astype(BF16)
    cum = _dot(onehot.astype(BF16), before) + carry_ref[...]
    r1 = jnp.sum(jnp.where(is_e1, cum, 0.0), axis=0, keepdims=True)
    r2 = jnp.sum(jnp.where(is_e2, cum, 0.0), axis=0, keepdims=True)
    carry = carry_ref[...] + jnp.sum(onehot, axis=1, keepdims=True)
    carry_ref[...] = carry
    cnt_ref[0] = carry

    seq_len = float(tiles_per_seq * tm)
    row8 = lax.broadcasted_iota(jnp.int32, (SUBLANES, tm), 0)
    pos = jnp.where(row8 == 0, (e1idx - N_GROUPS) * seq_len + r1, (e2idx - N_GROUPS) * seq_len + r2)
    dest_ref[0] = pos.astype(jnp.int32)
    rw_ref[0] = jnp.where(row8 == 0, w1, w2)


def _merge_call(x2, hm, hd, sgm, sgd, mod3, wbm, wbd, wout, n2w, wr, br, seq):
    n, d = x2.shape
    tm = TM_MERGE
    tiles_per_seq = seq // tm
    row = lambda w: pl.BlockSpec((tm, w), lambda i: (i, 0))
    consts = [wbm, wbd, wout, n2w, wr, br]
    per_seq = pl.BlockSpec((1, SUBLANES, tm), lambda i: (i // tiles_per_seq, 0, i % tiles_per_seq))
    return pl.pallas_call(
        functools.partial(_merge_kernel, tiles_per_seq),
        grid=(n // tm,),
        in_specs=[row(d), row(M_WIDTH), row(D_WIDTH), row(d), row(d),
                  pl.BlockSpec((1, 6, d), lambda i: (i // tiles_per_seq, 0, 0))]
                 + [_const_spec(a.shape) for a in consts],
        out_specs=[pl.BlockSpec((tm * SUBLANES, LANES), lambda i: (i, 0)),
                   pl.BlockSpec((tm * SUBLANES, LANES), lambda i: (i, 0)),
                   per_seq, per_seq, pl.BlockSpec((1, ROUTE_ROWS, 1), lambda i: (i, 0, 0))],
        out_shape=[jax.ShapeDtypeStruct((n * SUBLANES, LANES), F32), jax.ShapeDtypeStruct((n * SUBLANES, LANES), F32),
                   jax.ShapeDtypeStruct((n // seq, SUBLANES, seq), jnp.int32),
                   jax.ShapeDtypeStruct((n // seq, SUBLANES, seq), F32),
                   jax.ShapeDtypeStruct((n // tm, ROUTE_ROWS, 1), F32)],
        scratch_shapes=[pltpu.VMEM((ROUTE_ROWS, 1), F32)],
        compiler_params=pltpu.CompilerParams(dimension_semantics=("arbitrary",),
                                             vmem_limit_bytes=VMEM_LIMIT),
    )(x2, hm, hd, sgm, sgd, mod3, *consts)


def _moe_kernel(dest_hbm, rw_hbm, meta_hbm, h2_hbm, x1_hbm, mod_ref, wg_hbm, wu_hbm, wd_hbm, out_hbm,
                dest, rw, meta, inv, h2buf, acc, wg_buf, wu_buf, wd_buf, xst, yst, ost, sems):
    b = pl.program_id(0)
    t_blk = h2buf.shape[0] // SUBLANES - 1
    spare = 2 * t_blk
    r = MOE_R
    tile = lambda ref, tok: ref.at[pl.ds(pl.multiple_of(tok * SUBLANES, SUBLANES), SUBLANES), :]

    def weight_copies(e, slot):
        return (pltpu.make_async_copy(wg_hbm.at[e], wg_buf.at[slot], sems.at[5 + slot]),
                pltpu.make_async_copy(wu_hbm.at[e], wu_buf.at[slot], sems.at[7 + slot]),
                pltpu.make_async_copy(wd_hbm.at[e], wd_buf.at[slot], sems.at[9 + slot]))

    route_copies = (pltpu.make_async_copy(dest_hbm.at[b], dest, sems.at[0]),
                    pltpu.make_async_copy(rw_hbm.at[b], rw, sems.at[1]),
                    pltpu.make_async_copy(meta_hbm.at[b], meta, sems.at[2]))
    h2_copy = pltpu.make_async_copy(h2_hbm.at[b], h2buf.at[pl.ds(0, t_blk * SUBLANES)], sems.at[3])
    x1_copy = pltpu.make_async_copy(x1_hbm.at[b], acc.at[pl.ds(0, t_blk * SUBLANES)], sems.at[4])

    for cp in route_copies + (h2_copy, x1_copy) + weight_copies(0, 0):
        cp.start()
    xst[...] = jnp.zeros(xst.shape, F32)
    yst[...] = jnp.zeros(yst.shape, F32)
    tile(h2buf, t_blk)[...] = jnp.zeros((SUBLANES, LANES), F32)
    tile(acc, t_blk)[...] = jnp.zeros((SUBLANES, LANES), F32)
    for cp in route_copies:
        cp.wait()

    last = inv.shape[0] - 1

    def pad_body(e, c):
        for j in range(MOE_U - 1):
            inv[jnp.minimum(e * t_blk + meta[e] + j, last)] = spare
        return c

    lax.fori_loop(0, N_EXPERTS, pad_body, 0)

    def inv_body(u, c):
        for j in range(MOE_U):
            s = u * MOE_U + j
            inv[dest[s]] = s
        return c

    lax.fori_loop(0, spare // MOE_U, inv_body, 0)
    h2_copy.wait()
    x1_copy.wait()
    gate2 = mod_ref[0, 5:6, :]
    n_chunks = D_MODEL // LANES

    def expert_body(e, c):
        slot = e % 2
        for cp in weight_copies(e, slot):
            cp.wait()

        @pl.when(e + 1 < N_EXPERTS)
        def _():
            for cp in weight_copies(e + 1, 1 - slot):
                cp.start()

        seg = e * t_blk
        n = meta[e]

        def run_tile(base, count, rows):
            groups = (count + MOE_U - 1) // MOE_U

            def gather(u, c3):
                row0 = u * MOE_U
                pos0 = base + row0
                for j in range(MOE_U):
                    xst[pl.ds(row0 + j, SUBLANES, stride=MOE_RS), :] = tile(h2buf, inv[pos0 + j] >> 1)[...]
                return c3

            lax.fori_loop(0, groups, gather, 0)
            x = jnp.concatenate([xst[ch * MOE_RS:ch * MOE_RS + rows, :] for ch in range(n_chunks)], axis=1)
            xb = x.astype(BF16)
            g = _dot(xb, wg_buf[slot])
            act = (g * _sigmoid(g) * _dot(xb, wu_buf[slot])).astype(BF16)
            y = _dot(act, wd_buf[slot]) * gate2
            for ch in range(n_chunks):
                yst[ch * MOE_RS:ch * MOE_RS + rows, :] = y[:, ch * LANES:(ch + 1) * LANES]

            def scatter(u, c3):
                toks, vals = [], []
                row0 = u * MOE_U
                pos0 = base + row0
                for j in range(MOE_U):
                    v = inv[pos0 + j]
                    toks.append(v >> 1)
                    vals.append(tile(acc, v >> 1)[...]
                                + rw[v] * yst[pl.ds(row0 + j, SUBLANES, stride=MOE_RS), :])
                for tok, val in zip(toks, vals):
                    tile(acc, tok)[...] = val
                return c3

            lax.fori_loop(0, groups, scatter, 0)

        def full_tile(k, c2):
            run_tile(seg + k * r, r, r)
            return c2

        n_full = n // r
        lax.fori_loop(0, n_full, full_tile, 0)
        rem = n - n_full * r

        @pl.when(rem > r // 2)
        def _():
            run_tile(seg + n_full * r, rem, r)

        @pl.when((rem > 0) & (rem <= r // 2))
        def _():
            run_tile(seg + n_full * r, rem, r // 2)

        return c

    lax.fori_loop(0, N_EXPERTS, expert_body, 0)

    rows = ost.shape[1]
    n_pieces = t_blk // rows

    def out_copy(p):
        return pltpu.make_async_copy(ost.at[p % 2], out_hbm.at[b, pl.ds(p * rows, rows)], sems.at[11 + p % 2])

    for p in range(n_pieces):
        if p >= 2:
            out_copy(p - 2).wait()
        for ch in range(n_chunks):
            ost[p % 2, :, ch * LANES:(ch + 1) * LANES] = acc[pl.ds(p * rows * SUBLANES + ch, rows, stride=SUBLANES), :]
        out_copy(p).start()
    for p in range(max(0, n_pieces - 2), n_pieces):
        out_copy(p).wait()


def _moe_call(dest, rw, meta, h2, x1, mod3, wg, wu, wd):
    nb = h2.shape[0]
    t_blk = h2.shape[1] // SUBLANES
    d = D_MODEL
    tok_rows = (t_blk + 1) * SUBLANES
    any_spec = pl.BlockSpec(memory_space=pl.ANY)
    return pl.pallas_call(
        _moe_kernel,
        grid=(nb,),
        in_specs=[any_spec, any_spec, any_spec, any_spec, any_spec,
                  pl.BlockSpec((1, 6, d), lambda i: (i, 0, 0)), any_spec, any_spec, any_spec],
        out_specs=any_spec,
        out_shape=jax.ShapeDtypeStruct((nb, t_blk, d), F32),
        scratch_shapes=[pltpu.SMEM(dest.shape[1:], jnp.int32), pltpu.SMEM(rw.shape[1:], F32),
                        pltpu.SMEM(meta.shape[1:], jnp.int32), pltpu.SMEM((N_EXPERTS * t_blk,), jnp.int32),
                        pltpu.VMEM((tok_rows, LANES), F32), pltpu.VMEM((tok_rows, LANES), F32),
                        pltpu.VMEM((2, d, D_EXPERT), BF16), pltpu.VMEM((2, d, D_EXPERT), BF16),
                        pltpu.VMEM((2, D_EXPERT, d), BF16),
                        pltpu.VMEM((SUBLANES * MOE_RS, LANES), F32), pltpu.VMEM((SUBLANES * MOE_RS, LANES), F32),
                        pltpu.VMEM((2, min(MOE_OUT_ROWS, t_blk), d), F32),
                        pltpu.SemaphoreType.DMA((13,))],
        compiler_params=pltpu.CompilerParams(dimension_semantics=("arbitrary",),
                                             vmem_limit_bytes=VMEM_LIMIT),
    )(dest, rw, meta, h2, x1, mod3, wg, wu, wd)


def _rope_freq_row():
    inv = ROPE_THETA ** (-np.arange(0, ROPE_DIM, 2, dtype=np.float32) / ROPE_DIM)
    row = np.zeros((LANES,), np.float32)
    half = ROPE_DIM // 2
    for base in range(0, LANES, D_HEAD_DIM):
        row[base:base + half] = inv
        row[base + half:base + ROPE_DIM] = inv
    return jnp.asarray(row.reshape(1, LANES))


def _block_diag_ones():
    idx = np.arange(MXU_DIM) // D_HEAD_DIM
    return jnp.asarray((idx[:, None] == idx[None, :]).astype(np.float32), dtype=BF16)


def kernel(x, c, positions, w_ada, b_ada, norm1_w, w_in, b_igate, b_fgate, conv_w, conv_b, mlstm_norm_w, q_norm_w, k_norm_w, lam_q1, lam_k1, lam_q2, lam_k2, subln_w, w_br_m, w_br_d, w_out, norm2_w, w_rg, b_rg, w_re, b_re, w_gate, w_up, w_down):
    bsz, seq, d = x.shape
    n = bsz * seq
    x2 = x.reshape(n, d)
    pos2 = positions.reshape(n, 1)
    l = 0

    mod3 = _ada_call(c, w_ada[l], b_ada[l]).reshape(bsz, 6, d)

    wi = w_in[l]
    gate_lo = 4 * M_WIDTH
    gate_hi = gate_lo + 2 * M_HEADS
    w_packed = jnp.concatenate(
        [wi[:, :gate_lo].astype(BF16),
         jnp.pad(wi[:, gate_lo:gate_hi], ((0, 0), (0, LANES - 2 * M_HEADS))).astype(BF16),
         wi[:, gate_hi:].astype(BF16)], axis=1)
    gate_bias = jnp.pad(jnp.concatenate([b_igate[l], b_fgate[l]]), (0, LANES - 2 * M_HEADS)).reshape(1, LANES)
    reps = D_WIDTH // D_HEAD_DIM
    smalls = [conv_w[l], conv_b[l].reshape(1, -1), gate_bias,
              jnp.tile(q_norm_w[l], reps).reshape(1, D_WIDTH), jnp.tile(k_norm_w[l], reps).reshape(1, D_WIDTH),
              _rope_freq_row(), _block_diag_ones()]
    qm, km, vm, so, gates, qd, kd, vd, sgm, sgd = _in_call(
        x2, pos2, mod3, norm1_w[l].reshape(1, d), w_packed, smalls, seq)

    hm = _mlstm_call(qm, km, vm, so, gates, mlstm_norm_w[l].reshape(1, M_WIDTH), bsz, seq)

    lam_rows = jnp.stack([lam_q1[l], lam_k1[l], lam_q2[l], lam_k2[l]])
    hd = _attn_call(qd, kd, vd, lam_rows, subln_w[l].reshape(1, D_V_DIM), jnp.stack([q_norm_w[l], k_norm_w[l]]),
                    bsz, seq)

    w_r = jnp.pad(jnp.concatenate([w_rg[l], w_re[l]], axis=1), ((0, 0), (0, LANES - N_GROUPS - N_EXPERTS)))
    wr_hi = w_r.astype(BF16)
    wr_lo = (w_r - wr_hi.astype(F32)).astype(BF16)
    wr = jnp.concatenate([wr_hi, wr_lo], axis=1)
    b_r = jnp.pad(jnp.concatenate([b_rg[l], b_re[l]]), (0, LANES - N_GROUPS - N_EXPERTS)).reshape(1, LANES)
    x1, h2, dest, rw, cnt = _merge_call(x2, hm, hd, sgm, sgd, mod3, w_br_m[l].astype(BF16), w_br_d[l].astype(BF16),
                                        w_out[l].astype(BF16), norm2_w[l].reshape(1, d), wr, b_r, seq)

    tiles_per_seq = seq // TM_MERGE
    meta = cnt[tiles_per_seq - 1::tiles_per_seq, N_GROUPS:N_GROUPS + N_EXPERTS, 0].astype(jnp.int32)
    meta = jnp.pad(meta, ((0, 0), (0, LANES - N_EXPERTS)))
    interleave = lambda a: a[:, 0:2, :].transpose(0, 2, 1).reshape(bsz, 2 * seq)
    dest = interleave(dest)
    rw = jnp.pad(interleave(rw), ((0, 0), (0, LANES)))
    tok_tiles = (bsz, seq * SUBLANES, LANES)
    return _moe_call(dest, rw, meta, h2.reshape(tok_tiles), x1.reshape(tok_tiles), mod3,
                     w_gate[l].astype(BF16), w_up[l].astype(BF16), w_down[l].astype(BF16))
```
